```python
import jax, jax.numpy as jnp
from jax import lax
import numpy as np

D_MODEL = 1024
BATCH = 8
SEQ = 2048
DEPTH = 2

GRID_W = 64
CTX_LEN = 256
N_MIXERS = 2
N_HEADS = 16
QK_NOPE_DIM = 64
QK_ROPE_DIM = 32
V_HEAD_DIM = 64
Q_LORA_RANK = 512
KV_LORA_RANK = 256
ROPE_THETA = 10000.0
Q_BLOCK = 128
CONV_WIDTH = 31
N_EXPERTS = 64
TOP_K = 6
N_GROUPS = 8
TOPK_GROUPS = 4
EXPERT_DIM = 256
SHARED_DIM = 256
ROUTED_SCALE = 2.5
EXPERT_BLOCK = 128
NORM_EPS = 1e-6

kernel_name = "hybrid_mla_conformer_moe_dit"


def rmsnorm(x, w):
    x32 = x.astype(jnp.float32)
    y = x32 * lax.rsqrt(jnp.mean(x32 * x32, axis=-1, keepdims=True) + NORM_EPS)
    return (y * w.astype(jnp.float32)).astype(x.dtype)


def layernorm(x, w, b):
    x32 = x.astype(jnp.float32)
    mu = jnp.mean(x32, axis=-1, keepdims=True)
    var = jnp.mean(jnp.square(x32 - mu), axis=-1, keepdims=True)
    y = (x32 - mu) * lax.rsqrt(var + NORM_EPS)
    return (y * w.astype(jnp.float32) + b.astype(jnp.float32)).astype(x.dtype)


def modulate(h, shift, scale):
    return h * (1 + scale) + shift


def axial_rope_tables(n_tokens, dtype):
    rows = n_tokens // GRID_W
    row = jnp.repeat(jnp.arange(rows, dtype=jnp.float32), GRID_W)
    col = jnp.tile(jnp.arange(GRID_W, dtype=jnp.float32), rows)
    axis_dim = QK_ROPE_DIM // 2
    inv_freq = ROPE_THETA ** (-jnp.arange(0, axis_dim, 2, dtype=jnp.float32) / axis_dim)
    ang_r = row[:, None] * inv_freq
    ang_c = col[:, None] * inv_freq
    ang = jnp.concatenate([ang_r, ang_r, ang_c, ang_c], axis=-1)
    return jnp.cos(ang).astype(dtype), jnp.sin(ang).astype(dtype)


def _rotate_half(v):
    v1, v2 = jnp.split(v, 2, axis=-1)
    return jnp.concatenate([-v2, v1], axis=-1)


def apply_axial_rope(v, cos, sin):
    half = QK_ROPE_DIM // 2
    rot = jnp.concatenate([_rotate_half(v[..., :half]), _rotate_half(v[..., half:])], axis=-1)
    return v * cos + rot * sin


def mla_queries(h, wq_a, q_norm, wq_b, rope):
    B, L, _ = h.shape
    q = (rmsnorm(h @ wq_a, q_norm) @ wq_b).reshape(B, L, N_HEADS, QK_NOPE_DIM + QK_ROPE_DIM)
    if rope is not None:
        cos, sin = rope
        q_pe = apply_axial_rope(q[..., QK_NOPE_DIM:], cos[:, None, :], sin[:, None, :])
        q = jnp.concatenate([q[..., :QK_NOPE_DIM], q_pe], axis=-1)
    return q


def mla_keys_values(h, wkv_a, kv_norm, wkv_b, rope):
    B, L, _ = h.shape
    kv_a = h @ wkv_a
    c_kv, k_pe = kv_a[..., :KV_LORA_RANK], kv_a[..., KV_LORA_RANK:]
    if rope is not None:
        cos, sin = rope
        k_pe = apply_axial_rope(k_pe, cos, sin)
    kv = (rmsnorm(c_kv, kv_norm) @ wkv_b).reshape(B, L, N_HEADS, QK_NOPE_DIM + V_HEAD_DIM)
    k_nope, v = kv[..., :QK_NOPE_DIM], kv[..., QK_NOPE_DIM:]
    k_pe = jnp.broadcast_to(k_pe[:, :, None, :], (B, L, N_HEADS, QK_ROPE_DIM))
    return jnp.concatenate([k_nope, k_pe], axis=-1), v


def attend(q, k, v):
    scale = (QK_NOPE_DIM + QK_ROPE_DIM) ** -0.5
    s = jnp.einsum('bqhd,bkhd->bhqk', q, k) * scale
    p = jax.nn.softmax(s.astype(jnp.float32), axis=-1).astype(v.dtype)
    return jnp.einsum('bhqk,bkhd->bqhd', p, v)


def blocked_attend(q, k, v):
    B, L, H, dq = q.shape
    nb = L // Q_BLOCK
    qb = q.reshape(B, nb, Q_BLOCK, H, dq).transpose(1, 0, 2, 3, 4)
    ob = lax.map(lambda qq: attend(qq, k, v), qb)
    return ob.transpose(1, 0, 2, 3, 4).reshape(B, L, H * V_HEAD_DIM)


def conv_module(h, pw1_w, pw1_b, dw_w, dw_b, n_w, n_b, pw2_w, pw2_b):
    u = h @ pw1_w + pw1_b
    a, g = jnp.split(u, 2, axis=-1)
    u = a * jax.nn.sigmoid(g)
    pad = CONV_WIDTH // 2
    u = lax.conv_general_dilated(u, dw_w[:, None, :], window_strides=(1,), padding=[(pad, pad)],
                                 dimension_numbers=('NWC', 'WIO', 'NWC'),
                                 feature_group_count=u.shape[-1]) + dw_b
    u = jax.nn.silu(layernorm(u, n_w, n_b))
    return u @ pw2_w + pw2_b


def swiglu(x, w1, w3, w2):
    return (jax.nn.silu(x @ w1) * (x @ w3)) @ w2


def moe_ffn(xs, router_w, router_bias, exp_w1, exp_w3, exp_w2, sh_w1, sh_w3, sh_w2):
    T, D = xs.shape
    scores = jax.nn.sigmoid((xs @ router_w).astype(jnp.float32))
    sel = scores + router_bias.astype(jnp.float32)
    per_group = N_EXPERTS // N_GROUPS
    group_scores = lax.top_k(sel.reshape(T, N_GROUPS, per_group), 2)[0].sum(-1)
    gidx = lax.top_k(group_scores, TOPK_GROUPS)[1]
    gmask = jax.nn.one_hot(gidx, N_GROUPS, dtype=jnp.float32).sum(1) > 0
    emask = jnp.repeat(gmask, per_group, axis=1)
    _, idx = lax.top_k(jnp.where(emask, sel, -jnp.inf), TOP_K)
    w = jnp.take_along_axis(scores, idx, axis=1)
    w = (w / jnp.sum(w, axis=-1, keepdims=True) * ROUTED_SCALE).astype(xs.dtype)

    TK = T * TOP_K
    flat_e = idx.reshape(-1)
    flat_tok = jnp.broadcast_to(jnp.arange(T, dtype=jnp.int32)[:, None], (T, TOP_K)).reshape(-1)
    flat_w = w.reshape(-1)
    order = jnp.argsort(flat_e, stable=True)
    sorted_e = flat_e[order]
    counts = jnp.zeros((N_EXPERTS,), jnp.int32).at[flat_e].add(1)
    padded = (counts + EXPERT_BLOCK - 1) // EXPERT_BLOCK * EXPERT_BLOCK
    pad_end = jnp.cumsum(padded)
    pad_start = pad_end - padded
    sort_start = jnp.cumsum(counts) - counts
    rank = jnp.arange(TK, dtype=jnp.int32) - sort_start[sorted_e]
    dest = pad_start[sorted_e] + rank
    n_blocks = (TK + EXPERT_BLOCK - 1) // EXPERT_BLOCK + N_EXPERTS
    P = n_blocks * EXPERT_BLOCK
    slot_tok = jnp.full((P,), T, jnp.int32).at[dest].set(flat_tok[order])
    slot_w = jnp.zeros((P,), xs.dtype).at[dest].set(flat_w[order])
    blk_e = jnp.minimum(jnp.searchsorted(pad_end, jnp.arange(n_blocks, dtype=jnp.int32) * EXPERT_BLOCK,
                                         side='right'), N_EXPERTS - 1)
    xs_pad = jnp.concatenate([xs, jnp.zeros((1, D), xs.dtype)], axis=0)
    xg = xs_pad[slot_tok].reshape(n_blocks, EXPERT_BLOCK, D)

    def expert_block(args):
        xb, e = args
        return swiglu(xb, exp_w1[e], exp_w3[e], exp_w2[e])

    yg = lax.map(expert_block, (xg, blk_e)).reshape(P, D) * slot_w[:, None]
    routed = jax.ops.segment_sum(yg, slot_tok, num_segments=T + 1)[:T]
    return routed + swiglu(xs, sh_w1, sh_w3, sh_w2)


def setup_inputs(seed: int = 0) -> dict:
    key = jax.random.key(seed)
    ks = iter(jax.random.split(key, 40))
    f32 = jnp.float32
    D = D_MODEL
    n_mla = (DEPTH + N_MIXERS - 1) // N_MIXERS
    n_conv = DEPTH // N_MIXERS

    def nrm(shape, scale):
        return jax.random.normal(next(ks), shape, f32) * scale

    def gain(shape):
        return 1.0 + nrm(shape, 0.1)

    return {
        "x": nrm((BATCH, SEQ, D), 1.0),
        "c": nrm((BATCH, D), 1.0),
        "ctx": nrm((BATCH, CTX_LEN, D), 1.0),
        "c_ctx": nrm((D,), 1.0),
        "mod_w": nrm((DEPTH, D, 6 * D), 0.5 * D ** -0.5),
        "mod_b": nrm((DEPTH, 6 * D), 0.02),
        "norm_mix_w": gain((DEPTH, D)),
        "norm_ffn_w": gain((DEPTH, D)),
        "mla_wq_a": nrm((n_mla, D, Q_LORA_RANK), D ** -0.5),
        "mla_q_norm": gain((n_mla, Q_LORA_RANK)),
        "mla_wq_b": nrm((n_mla, Q_LORA_RANK, N_HEADS * (QK_NOPE_DIM + QK_ROPE_DIM)), Q_LORA_RANK ** -0.5),
        "mla_wkv_a": nrm((n_mla, D, KV_LORA_RANK + QK_ROPE_DIM), D ** -0.5),
        "mla_kv_norm": gain((n_mla, KV_LORA_RANK)),
        "mla_wkv_b": nrm((n_mla, KV_LORA_RANK, N_HEADS * (QK_NOPE_DIM + V_HEAD_DIM)), KV_LORA_RANK ** -0.5),
        "mla_wo": nrm((n_mla, N_HEADS * V_HEAD_DIM, D), (N_HEADS * V_HEAD_DIM) ** -0.5),
        "conv_pw1_w": nrm((n_conv, D, 2 * D), D ** -0.5),
        "conv_pw1_b": nrm((n_conv, 2 * D), 0.02),
        "conv_dw_w": nrm((n_conv, CONV_WIDTH, D), CONV_WIDTH ** -0.5),
        "conv_dw_b": nrm((n_conv, D), 0.02),
        "conv_norm_w": gain((n_conv, D)),
        "conv_norm_b": nrm((n_conv, D), 0.02),
        "conv_pw2_w": nrm((n_conv, D, D), D ** -0.5),
        "conv_pw2_b": nrm((n_conv, D), 0.02),
        "router_w": nrm((DEPTH, D, N_EXPERTS), D ** -0.5),
        "router_bias": nrm((DEPTH, N_EXPERTS), 0.01),
        "exp_w1": nrm((DEPTH, N_EXPERTS, D, EXPERT_DIM), D ** -0.5),
        "exp_w3": nrm((DEPTH, N_EXPERTS, D, EXPERT_DIM), D ** -0.5),
        "exp_w2": nrm((DEPTH, N_EXPERTS, EXPERT_DIM, D), EXPERT_DIM ** -0.5),
        "shared_w1": nrm((DEPTH, D, SHARED_DIM), D ** -0.5),
        "shared_w3": nrm((DEPTH, D, SHARED_DIM), D ** -0.5),
        "shared_w2": nrm((DEPTH, SHARED_DIM, D), SHARED_DIM ** -0.5),
        "final_norm_w": gain((D,)),
    }


def reference(x, c, ctx, c_ctx, mod_w, mod_b, norm_mix_w, norm_ffn_w,
              mla_wq_a, mla_q_norm, mla_wq_b, mla_wkv_a, mla_kv_norm, mla_wkv_b, mla_wo,
              conv_pw1_w, conv_pw1_b, conv_dw_w, conv_dw_b, conv_norm_w, conv_norm_b,
              conv_pw2_w, conv_pw2_b,
              router_w, router_bias, exp_w1, exp_w3, exp_w2, shared_w1, shared_w3, shared_w2,
              final_norm_w):
    B, L, D = x.shape
    C = ctx.shape[1]
    rope = axial_rope_tables(L, x.dtype)
    xs, cs = x, ctx
    for i in range(DEPTH):
        last = i == DEPTH - 1
        j = i // N_MIXERS
        is_mla = i % N_MIXERS == 0
        mx = jax.nn.silu(c) @ mod_w[i] + mod_b[i]
        sh1x, sc1x, g1x, sh2x, sc2x, g2x = [t[:, None, :] for t in jnp.split(mx, 6, axis=-1)]
        hx = modulate(rmsnorm(xs, norm_mix_w[i]), sh1x, sc1x)
        need_ctx = is_mla or not last
        if need_ctx:
            mc = jax.nn.silu(c_ctx) @ mod_w[i] + mod_b[i]
            sh1c, sc1c, g1c, sh2c, sc2c, g2c = jnp.split(mc, 6, axis=-1)
            hc = modulate(rmsnorm(cs, norm_mix_w[i]), sh1c, sc1c)

        if is_mla:
            k_c, v_c = mla_keys_values(hc, mla_wkv_a[j], mla_kv_norm[j], mla_wkv_b[j], None)
            k_x, v_x = mla_keys_values(hx, mla_wkv_a[j], mla_kv_norm[j], mla_wkv_b[j], rope)
            q_x = mla_queries(hx, mla_wq_a[j], mla_q_norm[j], mla_wq_b[j], rope)
            keys = jnp.concatenate([k_c, k_x], axis=1)
            vals = jnp.concatenate([v_c, v_x], axis=1)
            out_x = blocked_attend(q_x, keys, vals) @ mla_wo[j]
            if not last:
                q_c = mla_queries(hc, mla_wq_a[j], mla_q_norm[j], mla_wq_b[j], None)
                out_c = attend(q_c, k_c, v_c).reshape(B, C, N_HEADS * V_HEAD_DIM) @ mla_wo[j]
        else:
            conv_args = (conv_pw1_w[j], conv_pw1_b[j], conv_dw_w[j], conv_dw_b[j],
                         conv_norm_w[j], conv_norm_b[j], conv_pw2_w[j], conv_pw2_b[j])
            out_x = conv_module(hx, *conv_args)
            if not last:
                out_c = conv_module(hc, *conv_args)
        xs = xs + g1x * out_x

        moe_args = (router_w[i], router_bias[i], exp_w1[i], exp_w3[i], exp_w2[i],
                    shared_w1[i], shared_w3[i], shared_w2[i])
        h2x = modulate(rmsnorm(xs, norm_ffn_w[i]), sh2x, sc2x)
        if last:
            xs = xs + g2x * moe_ffn(h2x.reshape(B * L, D), *moe_args).reshape(B, L, D)
        else:
            cs = cs + g1c * out_c
            h2c = modulate(rmsnorm(cs, norm_ffn_w[i]), sh2c, sc2c)
            tokens = jnp.concatenate([h2x.reshape(B * L, D), h2c.reshape(B * C, D)], axis=0)
            y = moe_ffn(tokens, *moe_args)
            xs = xs + g2x * y[:B * L].reshape(B, L, D)
            cs = cs + g2c * y[B * L:].reshape(B, C, D)
    return rmsnorm(xs, final_norm_w)
```

```python
import functools

import jax
import jax.numpy as jnp
from jax import lax
from jax.experimental import pallas as pl
from jax.experimental.pallas import tpu as pltpu

N_HEADS = 16
QK_NOPE_DIM = 64
QK_ROPE_DIM = 32
V_HEAD_DIM = 64
Q_LORA_RANK = 512
KV_LORA_RANK = 256
ROPE_THETA = 10000.0
GRID_W = 64
CONV_WIDTH = 31
N_EXPERTS = 64
TOP_K = 6
N_GROUPS = 8
TOPK_GROUPS = 4
ROUTED_SCALE = 2.5
NORM_EPS = 1e-6

LANES = 128
SUBLANES = 8
TOKEN_TILE = 256
EXPERT_ROWS = 256
DISPATCH_TILE = 256
COMBINE_TILE = 128
CONV_HALO = 16
N_PAIRS = N_HEADS // 2
N_ROPE_GROUPS = N_HEADS // 4
VMEM_LIMIT = 56 * 1024 * 1024

F32 = jnp.float32
BF16 = jnp.bfloat16


def _cparams(sem):
    return pltpu.CompilerParams(dimension_semantics=sem, vmem_limit_bytes=VMEM_LIMIT)


def _rms(x, w):
    return x * lax.rsqrt(jnp.mean(x * x, axis=-1, keepdims=True) + NORM_EPS) * w


def _sigmoid(x):
    return 1.0 / (1.0 + jnp.exp(-x))


def _silu(x):
    return x * _sigmoid(x)


def _dot(a, b):
    return jnp.dot(a, b, preferred_element_type=F32)


def _mod_kernel(c_ref, w_ref, b_ref, o_ref):
    c = c_ref[...]
    o_ref[0] = jnp.dot(_silu(c), w_ref[0], preferred_element_type=F32,
                       precision=lax.Precision.HIGHEST) + b_ref[0]


def _modulation(cvec, mod_w, mod_b):
    depth, d, n = mod_w.shape
    rows = cvec.shape[0]
    tn = 1536
    return pl.pallas_call(
        _mod_kernel,
        grid=(depth, n // tn),
        in_specs=[pl.BlockSpec((rows, d), lambda l, j: (0, 0)),
                  pl.BlockSpec((1, d, tn), lambda l, j: (l, 0, j)),
                  pl.BlockSpec((1, 1, tn), lambda l, j: (l, 0, j))],
        out_specs=pl.BlockSpec((1, rows, tn), lambda l, j: (l, 0, j)),
        out_shape=jax.ShapeDtypeStruct((depth, rows, n), F32),
        compiler_params=_cparams(("arbitrary", "arbitrary")),
        name="modulation",
    )(cvec, mod_w, mod_b.reshape(depth, 1, n))


def _rope(v, cos, sin):
    n = v.shape[1]
    lane = lax.broadcasted_iota(jnp.int32, v.shape, 1)
    up = pltpu.roll(v, 8, axis=1)
    dn = pltpu.roll(v, n - 8, axis=1)
    rot = jnp.where(lane % 16 < 8, -dn, up)
    return v * cos + rot * sin


def _pre_mla_kernel(tok_ref, mod_ref, nw_ref, wqa_ref, qn_ref, wqb_ref, wkva_ref, kvn_ref,
                    wkvb_ref, cos_ref, sin_ref, qn_out, qp_out, k_out, v_out):
    x = tok_ref[...]
    m = mod_ref[0]
    h = _rms(x, nw_ref[...]) * (1.0 + m[1:2]) + m[0:1]
    hb = h.astype(BF16)
    cos = cos_ref[...]
    sin = sin_ref[...]

    qa = _rms(_dot(hb, wqa_ref[...]), qn_ref[...])
    scale = (QK_NOPE_DIM + QK_ROPE_DIM) ** -0.5
    q = _dot(qa.astype(BF16), wqb_ref[...]) * scale
    d_nope = N_HEADS * QK_NOPE_DIM
    q_pe = _rope(q[:, d_nope:], cos, sin)

    kva = _dot(hb, wkva_ref[...])
    kpe4 = _rope(kva[:, KV_LORA_RANK:], cos[:, :LANES], sin[:, :LANES]).astype(BF16)
    ckv = _rms(kva[:, :KV_LORA_RANK], kvn_ref[...])
    kv = _dot(ckv.astype(BF16), wkvb_ref[...])

    for j in range(N_PAIRS):
        sl = slice(j * LANES, (j + 1) * LANES)
        qn_out[j] = q[:, sl].astype(BF16)
        k_out[j, :, :LANES] = kv[:, sl].astype(BF16)
        k_out[j, :, LANES:] = kpe4
        v_out[j] = kv[:, d_nope + j * LANES:d_nope + (j + 1) * LANES].astype(BF16)
    for g in range(N_ROPE_GROUPS):
        qp_out[g] = q_pe[:, g * LANES:(g + 1) * LANES].astype(BF16)


def _pre_mla(tok, mod, nw, wqa, qn, wqb, wkva, kvn, wkvb, cos, sin, *, nb, tpb, cpb):
    t, d = tok.shape
    tm = TOKEN_TILE
    nx = nb * tpb
    spb = tpb + cpb

    def mrow(i):
        return jnp.where(i < nx, i // tpb, nb)

    def kvblk(i):
        ic = i - nx
        return jnp.where(i < nx, (i // tpb) * spb + cpb + i % tpb, (ic // cpb) * spb + ic % cpb)

    def ropeblk(i):
        return jnp.where(i < nx, i % tpb, tpb)

    full = lambda a: pl.BlockSpec(a.shape, lambda i: (0,) * a.ndim)
    s_rows = nb * spb * tm
    return pl.pallas_call(
        _pre_mla_kernel,
        grid=(t // tm,),
        in_specs=[pl.BlockSpec((tm, d), lambda i: (i, 0)),
                  pl.BlockSpec((1, 6, d), lambda i: (mrow(i), 0, 0)),
                  full(nw), full(wqa), full(qn), full(wqb), full(wkva), full(kvn), full(wkvb),
                  pl.BlockSpec((tm, cos.shape[1]), lambda i: (ropeblk(i), 0)),
                  pl.BlockSpec((tm, sin.shape[1]), lambda i: (ropeblk(i), 0))],
        out_specs=[pl.BlockSpec((N_PAIRS, tm, LANES), lambda i: (0, i, 0)),
                   pl.BlockSpec((N_ROPE_GROUPS, tm, LANES), lambda i: (0, i, 0)),
                   pl.BlockSpec((N_PAIRS, tm, 2 * LANES), lambda i: (0, kvblk(i), 0)),
                   pl.BlockSpec((N_PAIRS, tm, LANES), lambda i: (0, kvblk(i), 0))],
        out_shape=[jax.ShapeDtypeStruct((N_PAIRS, t, LANES), BF16),
                   jax.ShapeDtypeStruct((N_ROPE_GROUPS, t, LANES), BF16),
                   jax.ShapeDtypeStruct((N_PAIRS, s_rows, 2 * LANES), BF16),
                   jax.ShapeDtypeStruct((N_PAIRS, s_rows, LANES), BF16)],
        compiler_params=_cparams(("arbitrary",)),
        name="pre_mla",
    )(tok, mod, nw, wqa, qn, wqb, wkva, kvn, wkvb, cos, sin)


def _attn_tile(qn_ref, qp_ref, k_ref, v_ref, o_ref, s_len):
    tq = qn_ref.shape[1]
    lane = lax.broadcasted_iota(jnp.int32, (tq, LANES), 1)

    def pair(j, carry):
        qn = qn_ref[j]
        qp = qp_ref[j // 2]
        k = k_ref[j, :s_len, :]
        v = v_ref[j, :s_len, :]
        outs = []
        for e in range(2):
            qn_m = jnp.where(lane // QK_NOPE_DIM == e, qn, jnp.zeros_like(qn))
            qp_m = jnp.where(lane // QK_ROPE_DIM == 2 * (j % 2) + e, qp, jnp.zeros_like(qp))
            lhs = jnp.concatenate([qn_m, qp_m], axis=1)
            s = lax.dot_general(lhs, k, (((1,), (1,)), ((), ())), preferred_element_type=F32)
            p = jnp.exp(s - jnp.max(s, axis=-1, keepdims=True))
            l = jnp.sum(p, axis=-1, keepdims=True)
            outs.append(_dot(p.astype(BF16), v) * (1.0 / l))
        o_ref[j] = jnp.where(lane < V_HEAD_DIM, outs[0], outs[1]).astype(BF16)
        return carry

    lax.fori_loop(0, N_PAIRS, pair, 0)


def _attn_kernel(qn_ref, qp_ref, k_ref, v_ref, o_ref, *, tpb, n_ctx):
    step = pl.program_id(1)

    @pl.when(step < tpb)
    def _():
        _attn_tile(qn_ref, qp_ref, k_ref, v_ref, o_ref, k_ref.shape[1])

    @pl.when(step >= tpb)
    def _():
        _attn_tile(qn_ref, qp_ref, k_ref, v_ref, o_ref, n_ctx)


def _attention(qn, qp, kcat, v, *, nb, tpb, cpb):
    t = qn.shape[1]
    tq = TOKEN_TILE
    nx = nb * tpb
    s_len = (tpb + cpb) * tq

    def qblk(b, i):
        return jnp.where(i < tpb, b * tpb + i, nx + b * cpb + (i - tpb))

    return pl.pallas_call(
        functools.partial(_attn_kernel, tpb=tpb, n_ctx=cpb * tq),
        grid=(nb, tpb + cpb),
        in_specs=[pl.BlockSpec((N_PAIRS, tq, LANES), lambda b, i: (0, qblk(b, i), 0)),
                  pl.BlockSpec((N_ROPE_GROUPS, tq, LANES), lambda b, i: (0, qblk(b, i), 0)),
                  pl.BlockSpec((N_PAIRS, s_len, 2 * LANES), lambda b, i: (0, b, 0)),
                  pl.BlockSpec((N_PAIRS, s_len, LANES), lambda b, i: (0, b, 0))],
        out_specs=pl.BlockSpec((N_PAIRS, tq, LANES), lambda b, i: (0, qblk(b, i), 0)),
        out_shape=jax.ShapeDtypeStruct((N_PAIRS, t, LANES), BF16),
        compiler_params=_cparams(("arbitrary", "arbitrary")),
        name="attention",
    )(qn, qp, kcat, v)


def _post_kernel(act_ref, wmix_ref, bmix_ref, xs_ref, mod_ref, nfw_ref, rwt_ref, rb_ref,
                 sw1_ref, sw3_ref, sw2_ref,
                 h2_out, base_out, idx_out, wt_out, rank_out, cnt_out, cnt_scr, *, pair_major):
    i = pl.program_id(0)
    tm = xs_ref.shape[0]

    @pl.when(i == 0)
    def _():
        cnt_scr[...] = jnp.zeros_like(cnt_scr)

    if pair_major:
        act = jnp.concatenate([act_ref[j] for j in range(N_PAIRS)], axis=1)
    else:
        act = act_ref[...]
    m = mod_ref[0]
    mix = _dot(act, wmix_ref[...]) + bmix_ref[...]
    xs = xs_ref[...] + m[2:3] * mix
    h2 = _rms(xs, nfw_ref[...]) * (1.0 + m[4:5]) + m[3:4]
    h2_out[...] = h2

    h2b = h2.astype(BF16)
    hid = _silu(_dot(h2b, sw1_ref[...])) * _dot(h2b, sw3_ref[...])
    base_out[...] = xs + m[5:6] * _dot(hid.astype(BF16), sw2_ref[...])

    logits = lax.dot_general(rwt_ref[...], h2, (((1,), (1,)), ((), ())),
                             preferred_element_type=F32, precision=lax.Precision.HIGHEST)
    scores = _sigmoid(logits)
    sel = scores + rb_ref[...]
    per_group = N_EXPERTS // N_GROUPS
    neg = jnp.float32(-jnp.inf)
    sub_iota = lax.broadcasted_iota(jnp.int32, (per_group, tm), 0)
    gs_rows = []
    for g in range(N_GROUPS):
        sg = sel[g * per_group:(g + 1) * per_group, :]
        m1 = jnp.max(sg, axis=0, keepdims=True)
        first = jnp.min(jnp.where(sg == m1, sub_iota, per_group), axis=0, keepdims=True)
        m2 = jnp.max(jnp.where(sub_iota == first, neg, sg), axis=0, keepdims=True)
        gs_rows.append(m1 + m2)
    gs = jnp.concatenate(gs_rows, axis=0)
    g_iota = lax.broadcasted_iota(jnp.int32, (N_GROUPS, tm), 0)
    gmask = jnp.zeros((N_GROUPS, tm), F32)
    work = gs
    for _ in range(TOPK_GROUPS):
        mx = jnp.max(work, axis=0, keepdims=True)
        gi = jnp.min(jnp.where(work == mx, g_iota, N_GROUPS), axis=0, keepdims=True)
        pick = g_iota == gi
        gmask = jnp.where(pick, 1.0, gmask)
        work = jnp.where(pick, neg, work)
    emask = jnp.concatenate(
        [jnp.broadcast_to(gmask[g:g + 1, :], (per_group, tm)) for g in range(N_GROUPS)], axis=0)
    masked = jnp.where(emask > 0.5, sel, neg)
    e_iota = lax.broadcasted_iota(jnp.int32, (N_EXPERTS, tm), 0)
    picks, idx_rows, w_rows = [], [], []
    for _ in range(TOP_K):
        mx = jnp.max(masked, axis=0, keepdims=True)
        ei = jnp.min(jnp.where(masked == mx, e_iota, N_EXPERTS), axis=0, keepdims=True)
        pick = e_iota == ei
        picks.append(pick)
        idx_rows.append(ei)
        w_rows.append(jnp.sum(jnp.where(pick, scores, 0.0), axis=0, keepdims=True))
        masked = jnp.where(pick, neg, masked)
    wsum = w_rows[0]
    for r in w_rows[1:]:
        wsum = wsum + r
    wnorm = ROUTED_SCALE / wsum

    member = jnp.zeros((N_EXPERTS, tm), F32)
    for pick in picks:
        member = jnp.where(pick, 1.0, member)
    before = (lax.broadcasted_iota(jnp.int32, (tm, tm), 0)
              < lax.broadcasted_iota(jnp.int32, (tm, tm), 1))
    prefix = _dot(member.astype(BF16), jnp.where(before, 1.0, 0.0).astype(BF16))
    cnt = cnt_scr[...]
    rank_full = cnt[:, 0:1] + prefix
    rank_rows = [jnp.sum(jnp.where(pick, rank_full, 0.0), axis=0, keepdims=True)
                 for pick in picks]
    cnt = cnt + jnp.sum(member, axis=1, keepdims=True)
    cnt_scr[...] = cnt
    cnt_out[...] = cnt

    pad = SUBLANES - TOP_K
    zi = jnp.zeros((pad, tm), jnp.int32)
    zf = jnp.zeros((pad, tm), F32)
    idx_out[...] = jnp.concatenate(idx_rows + [zi], axis=0)
    wt_out[...] = jnp.concatenate([r * wnorm for r in w_rows] + [zf], axis=0)
    rank_out[...] = jnp.concatenate([r.astype(jnp.int32) for r in rank_rows] + [zi], axis=0)


def _post(act, wmix, bmix, xs, mod, nfw, rwt, rb, sw1, sw3, sw2, *, nt, mrow, pair_major):
    d = xs.shape[1]
    tm = TOKEN_TILE
    t = nt * tm
    full = lambda a: pl.BlockSpec(a.shape, lambda i: (0,) * a.ndim)
    if pair_major:
        act_spec = pl.BlockSpec((N_PAIRS, tm, LANES), lambda i: (0, i, 0))
    else:
        act_spec = pl.BlockSpec((tm, act.shape[1]), lambda i: (i, 0))
    row8 = pl.BlockSpec((SUBLANES, tm), lambda i: (0, i))
    return pl.pallas_call(
        functools.partial(_post_kernel, pair_major=pair_major),
        grid=(nt,),
        in_specs=[act_spec, full(wmix), full(bmix),
                  pl.BlockSpec((tm, d), lambda i: (i, 0)),
                  pl.BlockSpec((1, 6, d), lambda i: (mrow(i), 0, 0)),
                  full(nfw), full(rwt), full(rb), full(sw1), full(sw3), full(sw2)],
        out_specs=[pl.BlockSpec((tm, d), lambda i: (i, 0)),
                   pl.BlockSpec((tm, d), lambda i: (i, 0)),
                   row8, row8, row8,
                   pl.BlockSpec((N_EXPERTS, LANES), lambda i: (0, 0))],
        out_shape=[jax.ShapeDtypeStruct((t, d), F32),
                   jax.ShapeDtypeStruct((t, d), F32),
                   jax.ShapeDtypeStruct((SUBLANES, t), jnp.int32),
                   jax.ShapeDtypeStruct((SUBLANES, t), F32),
                   jax.ShapeDtypeStruct((SUBLANES, t), jnp.int32),
                   jax.ShapeDtypeStruct((N_EXPERTS, LANES), F32)],
        scratch_shapes=[pltpu.VMEM((N_EXPERTS, LANES), F32)],
        compiler_params=_cparams(("arbitrary",)),
        name="post_mixer",
    )(act, wmix, bmix, xs, mod, nfw, rwt, rb, sw1, sw3, sw2)


def _dispatch_kernel(dest_ref, h2_hbm, xg_in, xg_out, sem, *, t_total, tile):
    del xg_in
    base = pl.program_id(0) * tile

    def row_copy(tok, slot):
        return pltpu.make_async_copy(h2_hbm.at[pl.ds(tok, 1)], xg_out.at[pl.ds(slot, 1)], sem)

    def issue(r, carry):
        tok = base + r
        for k in range(TOP_K):
            row_copy(tok, dest_ref[k * t_total + tok]).start()
        return carry

    lax.fori_loop(0, tile, issue, 0)

    def drain(r, carry):
        for k in range(TOP_K):
            row_copy(0, 0).wait()
        return carry

    lax.fori_loop(0, tile, drain, 0)


def _dispatch(dest, h2, n_slots):
    t, d = h2.shape
    tile = DISPATCH_TILE
    xg0 = jnp.zeros((n_slots, d), h2.dtype)
    return pl.pallas_call(
        functools.partial(_dispatch_kernel, t_total=t, tile=tile),
        grid_spec=pltpu.PrefetchScalarGridSpec(
            num_scalar_prefetch=1,
            grid=(t // tile,),
            in_specs=[pl.BlockSpec(memory_space=pl.ANY), pl.BlockSpec(memory_space=pl.ANY)],
            out_specs=pl.BlockSpec(memory_space=pl.ANY),
            scratch_shapes=[pltpu.SemaphoreType.DMA(())]),
        out_shape=jax.ShapeDtypeStruct((n_slots, d), h2.dtype),
        input_output_aliases={2: 0},
        compiler_params=_cparams(("arbitrary",)),
        name="moe_dispatch",
    )(dest, h2, xg0)


def _expert_kernel(be_ref, nu_ref, x_ref, w1_ref, w3_ref, w2_ref, y_ref):
    del be_ref

    @pl.when(pl.program_id(0) < nu_ref[0])
    def _():
        x = x_ref[...].astype(BF16)
        hid = _silu(_dot(x, w1_ref[0])) * _dot(x, w3_ref[0])
        y_ref[...] = _dot(hid.astype(BF16), w2_ref[0])


def _experts(blk_e, n_used, xg, w1, w3, w2):
    p, d = xg.shape
    rows = EXPERT_ROWS
    nblk = p // rows
    e_dim = w1.shape[2]

    def xmap(n, be, nu):
        return (jnp.minimum(n, nu[0] - 1), 0)

    def wmap(n, be, nu):
        return (be[n], 0, 0)

    return pl.pallas_call(
        _expert_kernel,
        grid_spec=pltpu.PrefetchScalarGridSpec(
            num_scalar_prefetch=2,
            grid=(nblk,),
            in_specs=[pl.BlockSpec((rows, d), xmap),
                      pl.BlockSpec((1, d, e_dim), wmap),
                      pl.BlockSpec((1, d, e_dim), wmap),
                      pl.BlockSpec((1, e_dim, d), wmap)],
            out_specs=pl.BlockSpec((rows, d), xmap)),
        out_shape=jax.ShapeDtypeStruct((p, d), F32),
        compiler_params=_cparams(("arbitrary",)),
        name="moe_experts",
    )(blk_e, n_used, xg, w1, w3, w2)


def _combine_kernel(dest_ref, y_hbm, base_ref, wt_ref, mod_ref, fnw_ref, o_ref, buf, sem,
                    *, t_total, final_norm):
    tm = base_ref.shape[0]
    base = pl.program_id(0) * tm

    def row_copy(slot, k, r):
        return pltpu.make_async_copy(y_hbm.at[pl.ds(slot, 1)], buf.at[k, pl.ds(r, 1)], sem)

    def issue(r, carry):
        for k in range(TOP_K):
            row_copy(dest_ref[k * t_total + base + r], k, r).start()
        return carry

    lax.fori_loop(0, tm, issue, 0)

    def drain(r, carry):
        for k in range(TOP_K):
            row_copy(0, k, 0).wait()
        return carry

    lax.fori_loop(0, tm, drain, 0)

    wt = wt_ref[...]
    acc = wt[:, 0:1] * buf[0]
    for k in range(1, TOP_K):
        acc = acc + wt[:, k:k + 1] * buf[k]
    out = base_ref[...] + mod_ref[0][5:6] * acc
    if final_norm:
        out = _rms(out, fnw_ref[...])
    o_ref[...] = out


def _combine(dest, yg, base, wt_t, mod, fnw, *, nt, mrow, final_norm):
    d = base.shape[1]
    tm = COMBINE_TILE
    t = nt * tm
    return pl.pallas_call(
        functools.partial(_combine_kernel, t_total=base.shape[0], final_norm=final_norm),
        grid_spec=pltpu.PrefetchScalarGridSpec(
            num_scalar_prefetch=1,
            grid=(nt,),
            in_specs=[pl.BlockSpec(memory_space=pl.ANY),
                      pl.BlockSpec((tm, d), lambda i, dr: (i, 0)),
                      pl.BlockSpec((tm, SUBLANES), lambda i, dr: (i, 0)),
                      pl.BlockSpec((1, 6, d), lambda i, dr: (mrow(i), 0, 0)),
                      pl.BlockSpec((1, d), lambda i, dr: (0, 0))],
            out_specs=pl.BlockSpec((tm, d), lambda i, dr: (i, 0)),
            scratch_shapes=[pltpu.VMEM((TOP_K, tm, d), F32), pltpu.SemaphoreType.DMA(())]),
        out_shape=jax.ShapeDtypeStruct((t, d), F32),
        compiler_params=_cparams(("arbitrary",)),
        name="moe_combine",
    )(dest, yg, base, wt_t, mod, fnw)


def _moe(h2, base, idx8, wt8, rank8, cnt, mod, fnw, w1, w3, w2, *, nt_out, mrow, final_norm):
    t, d = h2.shape
    rows = EXPERT_ROWS
    counts = cnt[:, 0].astype(jnp.int32)
    padded = (counts + rows - 1) // rows * rows
    pad_end = jnp.cumsum(padded)
    pad_start = pad_end - padded
    nblk = (t * TOP_K + rows - 1) // rows + N_EXPERTS
    dest = (jnp.take(pad_start, idx8[:TOP_K], axis=0) + rank8[:TOP_K]).reshape(-1)
    n_used = (pad_end[-1] // rows).astype(jnp.int32)
    blk = jnp.minimum(jnp.arange(nblk, dtype=jnp.int32), n_used - 1)
    blk_e = jnp.minimum(jnp.searchsorted(pad_end, blk * rows, side='right'),
                        N_EXPERTS - 1).astype(jnp.int32)
    xg = _dispatch(dest, h2, nblk * rows)
    yg = _experts(blk_e, n_used.reshape(1), xg, w1, w3, w2)
    tile_ratio = TOKEN_TILE // COMBINE_TILE
    return _combine(dest, yg, base, wt8.T, mod, fnw, nt=nt_out * tile_ratio,
                    mrow=lambda i: mrow(i // tile_ratio), final_norm=final_norm)


def _pre_conv_kernel(xs_ref, mod_ref, nw_ref, w_ref, b_ref, o_ref):
    m = mod_ref[0]
    h = _rms(xs_ref[...], nw_ref[...]) * (1.0 + m[1:2]) + m[0:1]
    u = _dot(h.astype(BF16), w_ref[...]) + b_ref[...]
    d = o_ref.shape[1]
    o_ref[...] = u[:, :d] * _sigmoid(u[:, d:])


def _pre_conv(xs, mod, nw, w, b, *, nt, mrow):
    d = xs.shape[1]
    tm = TOKEN_TILE
    full = lambda a: pl.BlockSpec(a.shape, lambda i: (0,) * a.ndim)
    return pl.pallas_call(
        _pre_conv_kernel,
        grid=(nt,),
        in_specs=[pl.BlockSpec((tm, d), lambda i: (i, 0)),
                  pl.BlockSpec((1, 6, d), lambda i: (mrow(i), 0, 0)),
                  full(nw), full(w), full(b)],
        out_specs=pl.BlockSpec((tm, d), lambda i: (i, 0)),
        out_shape=jax.ShapeDtypeStruct((nt * tm, d), F32),
        compiler_params=_cparams(("arbitrary",)),
        name="pre_conv",
    )(xs, mod, nw, w, b)


def _conv_kernel(prev_ref, cur_ref, next_ref, dw_ref, db_ref, lw_ref, lb_ref, o_ref, win, conv,
                 *, tpb):
    tm, d = cur_ref.shape
    r = pl.program_id(0) % tpb
    halo = CONV_HALO
    pad = CONV_WIDTH // 2
    win[0:halo, :] = jnp.where(r > 0, prev_ref[...], 0.0)
    win[halo:halo + tm, :] = cur_ref[...]
    win[halo + tm:halo + tm + halo, :] = jnp.where(r < tpb - 1, next_ref[...], 0.0)

    rows = 64
    span = 2 * halo - SUBLANES
    for c in range(d // LANES):
        cs = slice(c * LANES, (c + 1) * LANES)
        taps = dw_ref[:, cs]
        bias = db_ref[:, cs]

        def chunk(q, carry):
            r0 = pl.multiple_of(q * rows, rows)
            slab = win[pl.ds(r0, rows + 2 * halo), cs]
            acc = jnp.zeros((rows, LANES), F32) + bias
            for b in range(SUBLANES):
                shifted = slab[b:b + rows + span, :]
                for j in range(CONV_WIDTH):
                    off = halo - pad + j
                    if off % SUBLANES == b:
                        a = off - b
                        acc = acc + taps[j:j + 1, :] * shifted[a:a + rows, :]
            conv[pl.ds(r0, rows), cs] = acc
            return carry

        lax.fori_loop(0, tm // rows, chunk, 0)

    u = conv[...]
    mu = jnp.mean(u, axis=-1, keepdims=True)
    var = jnp.mean(jnp.square(u - mu), axis=-1, keepdims=True)
    y = (u - mu) * lax.rsqrt(var + NORM_EPS) * lw_ref[...] + lb_ref[...]
    o_ref[...] = _silu(y).astype(BF16)


def _conv(glu, dw, db, lw, lb, *, nt, tpb):
    d = glu.shape[1]
    tm = TOKEN_TILE
    hb = tm // CONV_HALO
    n_halo_blocks = glu.shape[0] // CONV_HALO
    full = lambda a: pl.BlockSpec(a.shape, lambda i: (0,) * a.ndim)
    return pl.pallas_call(
        functools.partial(_conv_kernel, tpb=tpb),
        grid=(nt,),
        in_specs=[pl.BlockSpec((CONV_HALO, d), lambda i: (jnp.maximum(i * hb - 1, 0), 0)),
                  pl.BlockSpec((tm, d), lambda i: (i, 0)),
                  pl.BlockSpec((CONV_HALO, d),
                               lambda i: (jnp.minimum((i + 1) * hb, n_halo_blocks - 1), 0)),
                  full(dw), full(db), full(lw), full(lb)],
        out_specs=pl.BlockSpec((tm, d), lambda i: (i, 0)),
        out_shape=jax.ShapeDtypeStruct((nt * tm, d), BF16),
        scratch_shapes=[pltpu.VMEM((tm + 2 * CONV_HALO, d), F32), pltpu.VMEM((tm, d), F32)],
        compiler_params=_cparams(("arbitrary",)),
        name="dw_conv",
    )(glu, glu, glu, dw, db, lw, lb)


def _rope_tables(n_tokens, extra_rows):
    rows = n_tokens // GRID_W
    row = jnp.repeat(jnp.arange(rows, dtype=F32), GRID_W)
    col = jnp.tile(jnp.arange(GRID_W, dtype=F32), rows)
    axis_dim = QK_ROPE_DIM // 2
    inv_freq = ROPE_THETA ** (-jnp.arange(0, axis_dim, 2, dtype=F32) / axis_dim)
    ang_r = row[:, None] * inv_freq
    ang_c = col[:, None] * inv_freq
    ang = jnp.concatenate([ang_r, ang_r, ang_c, ang_c], axis=-1)
    cos = jnp.concatenate([jnp.cos(ang), jnp.ones((extra_rows, QK_ROPE_DIM), F32)], axis=0)
    sin = jnp.concatenate([jnp.sin(ang), jnp.zeros((extra_rows, QK_ROPE_DIM), F32)], axis=0)
    return jnp.tile(cos, (1, N_HEADS)), jnp.tile(sin, (1, N_HEADS))


def kernel(x, c, ctx, c_ctx, mod_w, mod_b, norm_mix_w, norm_ffn_w, mla_wq_a, mla_q_norm, mla_wq_b, mla_wkv_a, mla_kv_norm, mla_wkv_b, mla_wo, conv_pw1_w, conv_pw1_b, conv_dw_w, conv_dw_b, conv_norm_w, conv_norm_b, conv_pw2_w, conv_pw2_b, router_w, router_bias, exp_w1, exp_w3, exp_w2, shared_w1, shared_w3, shared_w2, final_norm_w):
    nb, seq, d = x.shape
    n_ctx = ctx.shape[1]
    tm = TOKEN_TILE
    tpb, cpb = seq // tm, n_ctx // tm
    nx, nc = nb * tpb, nb * cpb
    row = lambda a: a.reshape(1, -1)

    pad_rows = (-(nb + 1)) % SUBLANES
    cvec = jnp.concatenate([c, c_ctx[None, :], jnp.zeros((pad_rows, d), F32)], axis=0)
    mod = _modulation(cvec, mod_w, mod_b).reshape(mod_w.shape[0], cvec.shape[0], 6, d)

    def mrow_all(i):
        return jnp.where(i < nx, i // tpb, nb)

    tok = jnp.concatenate([x.reshape(nb * seq, d), ctx.reshape(nb * n_ctx, d)], axis=0)
    wqb = mla_wq_b[0].reshape(Q_LORA_RANK, N_HEADS, QK_NOPE_DIM + QK_ROPE_DIM)
    wqb = jnp.concatenate([wqb[:, :, :QK_NOPE_DIM].reshape(Q_LORA_RANK, -1),
                           wqb[:, :, QK_NOPE_DIM:].reshape(Q_LORA_RANK, -1)], axis=1)
    wkva = mla_wkv_a[0]
    wkva = jnp.concatenate([wkva[:, :KV_LORA_RANK]]
                           + [wkva[:, KV_LORA_RANK:]] * (LANES // QK_ROPE_DIM), axis=1)
    wkvb = mla_wkv_b[0].reshape(KV_LORA_RANK, N_HEADS, QK_NOPE_DIM + V_HEAD_DIM)
    wkvb = jnp.concatenate([wkvb[:, :, :QK_NOPE_DIM].reshape(KV_LORA_RANK, -1),
                            wkvb[:, :, QK_NOPE_DIM:].reshape(KV_LORA_RANK, -1)], axis=1)
    cos, sin = _rope_tables(seq, tm)
    qn, qp, kcat, v = _pre_mla(
        tok, mod[0], row(norm_mix_w[0]), mla_wq_a[0].astype(BF16), row(mla_q_norm[0]),
        wqb.astype(BF16), wkva.astype(BF16), row(mla_kv_norm[0]), wkvb.astype(BF16), cos, sin,
        nb=nb, tpb=tpb, cpb=cpb)
    o = _attention(qn, qp, kcat, v, nb=nb, tpb=tpb, cpb=cpb)

    moe_w = lambda i: (exp_w1[i].astype(BF16), exp_w3[i].astype(BF16), exp_w2[i].astype(BF16))
    post_w = lambda i: (row(norm_ffn_w[i]), router_w[i].T, router_bias[i].reshape(-1, 1),
                        shared_w1[i].astype(BF16), shared_w3[i].astype(BF16),
                        shared_w2[i].astype(BF16))
    h2, base, idx8, wt8, rank8, cnt = _post(
        o, mla_wo[0].astype(BF16), jnp.zeros((1, d), F32), tok, mod[0], *post_w(0),
        nt=nx + nc, mrow=mrow_all, pair_major=True)
    xs = _moe(h2, base, idx8, wt8, rank8, cnt, mod[0], row(final_norm_w), *moe_w(0),
              nt_out=nx + nc, mrow=mrow_all, final_norm=False)

    mrow_x = lambda i: i // tpb
    glu = _pre_conv(xs, mod[1], row(norm_mix_w[1]), conv_pw1_w[0].astype(BF16),
                    row(conv_pw1_b[0]), nt=nx, mrow=mrow_x)
    taps = jnp.concatenate([conv_dw_w[0], jnp.zeros((1, d), F32)], axis=0)
    act = _conv(glu, taps, row(conv_dw_b[0]), row(conv_norm_w[0]), row(conv_norm_b[0]),
                nt=nx, tpb=tpb)
    h2, base, idx8, wt8, rank8, cnt = _post(
        act, conv_pw2_w[0].astype(BF16), row(conv_pw2_b[0]), xs, mod[1],
        *post_w(1), nt=nx, mrow=mrow_x, pair_major=False)
    out = _moe(h2, base, idx8, wt8, rank8, cnt, mod[1], row(final_norm_w), *moe_w(1),
               nt_out=nx, mrow=mrow_x, final_norm=True)
    return out.reshape(nb, seq, d)
```

```python
import functools

import jax
import jax.numpy as jnp
from jax import lax
from jax.experimental import pallas as pl
from jax.experimental.pallas import tpu as pltpu

N_HEADS = 16
QK_NOPE_DIM = 64
QK_ROPE_DIM = 32
V_HEAD_DIM = 64
Q_LORA_RANK = 512
KV_LORA_RANK = 256
ROPE_THETA = 10000.0
GRID_W = 64
CONV_WIDTH = 31
N_EXPERTS = 64
TOP_K = 6
N_GROUPS = 8
TOPK_GROUPS = 4
ROUTED_SCALE = 2.5
NORM_EPS = 1e-6
LOG2_E = 1.4426950408889634

LANES = 128
SUBLANES = 8
TOKEN_TILE = 256
EXPERT_ROWS = 256
DISPATCH_TILE = 256
COMBINE_TILE = 128
ISSUE_UNROLL = 4
CONV_HALO = 16
N_PAIRS = N_HEADS // 2
N_ROPE_GROUPS = N_HEADS // 4
VMEM_LIMIT = 56 * 1024 * 1024

F32 = jnp.float32
BF16 = jnp.bfloat16


def _cparams(sem):
    return pltpu.CompilerParams(dimension_semantics=sem, vmem_limit_bytes=VMEM_LIMIT)


def _rms(x, w):
    return x * lax.rsqrt(jnp.mean(x * x, axis=-1, keepdims=True) + NORM_EPS) * w


def _sigmoid(x):
    return 1.0 / (1.0 + jnp.exp(-x))


def _silu(x):
    return x * _sigmoid(x)


def _dot(a, b):
    return jnp.dot(a, b, preferred_element_type=F32)


def _pack_bf16_pairs(x):
    n = x.shape[1] // 2
    hi = lax.bitcast_convert_type(x[:, :n].astype(BF16).astype(F32), jnp.uint32)
    lo = lax.bitcast_convert_type(x[:, n:].astype(BF16).astype(F32), jnp.uint32)
    return hi | (lo >> 16)


def _unpack_bf16_pairs(p):
    hi = lax.bitcast_convert_type(p & jnp.uint32(0xFFFF0000), F32)
    lo = lax.bitcast_convert_type(p << 16, F32)
    return jnp.concatenate([hi, lo], axis=1)


def _mod_kernel(c_ref, w_ref, b_ref, o_ref):
    c = c_ref[...]
    o_ref[0] = jnp.dot(_silu(c), w_ref[0], preferred_element_type=F32,
                       precision=lax.Precision.HIGHEST) + b_ref[0]


def _modulation(cvec, mod_w, mod_b):
    depth, d, n = mod_w.shape
    rows = cvec.shape[0]
    tn = 1536
    return pl.pallas_call(
        _mod_kernel,
        grid=(depth, n // tn),
        in_specs=[pl.BlockSpec((rows, d), lambda l, j: (0, 0)),
                  pl.BlockSpec((1, d, tn), lambda l, j: (l, 0, j)),
                  pl.BlockSpec((1, 1, tn), lambda l, j: (l, 0, j))],
        out_specs=pl.BlockSpec((1, rows, tn), lambda l, j: (l, 0, j)),
        out_shape=jax.ShapeDtypeStruct((depth, rows, n), F32),
        compiler_params=_cparams(("arbitrary", "arbitrary")),
        name="modulation",
    )(cvec, mod_w, mod_b.reshape(depth, 1, n))


def _rope(v, cos, sin):
    n = v.shape[1]
    lane = lax.broadcasted_iota(jnp.int32, v.shape, 1)
    up = pltpu.roll(v, 8, axis=1)
    dn = pltpu.roll(v, n - 8, axis=1)
    rot = jnp.where(lane % 16 < 8, -dn, up)
    return v * cos + rot * sin


def _pre_mla_kernel(tok_ref, mod_ref, nw_ref, wqa_ref, qn_ref, wqb_ref, wkva_ref, kvn_ref,
                    wkvb_ref, cos_ref, sin_ref, qn_out, qp_out, k_out, v_out):
    x = tok_ref[...]
    m = mod_ref[0]
    h = _rms(x, nw_ref[...]) * (1.0 + m[1:2]) + m[0:1]
    hb = h.astype(BF16)
    cos = cos_ref[...]
    sin = sin_ref[...]

    qa = _rms(_dot(hb, wqa_ref[...]), qn_ref[...])
    scale = (QK_NOPE_DIM + QK_ROPE_DIM) ** -0.5 * LOG2_E
    q = _dot(qa.astype(BF16), wqb_ref[...]) * scale
    d_nope = N_HEADS * QK_NOPE_DIM
    q_pe = _rope(q[:, d_nope:], cos, sin)

    kva = _dot(hb, wkva_ref[...])
    kpe4 = _rope(kva[:, KV_LORA_RANK:], cos[:, :LANES], sin[:, :LANES]).astype(BF16)
    ckv = _rms(kva[:, :KV_LORA_RANK], kvn_ref[...])
    kv = _dot(ckv.astype(BF16), wkvb_ref[...])

    for j in range(N_PAIRS):
        sl = slice(j * LANES, (j + 1) * LANES)
        qn_out[j] = q[:, sl].astype(BF16)
        k_out[j, :, :LANES] = kv[:, sl].astype(BF16)
        k_out[j, :, LANES:] = kpe4
        v_out[j] = kv[:, d_nope + j * LANES:d_nope + (j + 1) * LANES].astype(BF16)
    for g in range(N_ROPE_GROUPS):
        qp_out[g] = q_pe[:, g * LANES:(g + 1) * LANES].astype(BF16)


def _pre_mla(tok, mod, nw, wqa, qn, wqb, wkva, kvn, wkvb, cos, sin, *, nb, tpb, cpb):
    t, d = tok.shape
    tm = TOKEN_TILE
    nx = nb * tpb
    spb = tpb + cpb

    def mrow(i):
        return jnp.where(i < nx, i // tpb, nb)

    def kvblk(i):
        ic = i - nx
        return jnp.where(i < nx, (i // tpb) * spb + cpb + i % tpb, (ic // cpb) * spb + ic % cpb)

    def ropeblk(i):
        return jnp.where(i < nx, i % tpb, tpb)

    full = lambda a: pl.BlockSpec(a.shape, lambda i: (0,) * a.ndim)
    s_rows = nb * spb * tm
    return pl.pallas_call(
        _pre_mla_kernel,
        grid=(t // tm,),
        in_specs=[pl.BlockSpec((tm, d), lambda i: (i, 0)),
                  pl.BlockSpec((1, 6, d), lambda i: (mrow(i), 0, 0)),
                  full(nw), full(wqa), full(qn), full(wqb), full(wkva), full(kvn), full(wkvb),
                  pl.BlockSpec((tm, cos.shape[1]), lambda i: (ropeblk(i), 0)),
                  pl.BlockSpec((tm, sin.shape[1]), lambda i: (ropeblk(i), 0))],
        out_specs=[pl.BlockSpec((N_PAIRS, tm, LANES), lambda i: (0, i, 0)),
                   pl.BlockSpec((N_ROPE_GROUPS, tm, LANES), lambda i: (0, i, 0)),
                   pl.BlockSpec((N_PAIRS, tm, 2 * LANES), lambda i: (0, kvblk(i), 0)),
                   pl.BlockSpec((N_PAIRS, tm, LANES), lambda i: (0, kvblk(i), 0))],
        out_shape=[jax.ShapeDtypeStruct((N_PAIRS, t, LANES), BF16),
                   jax.ShapeDtypeStruct((N_ROPE_GROUPS, t, LANES), BF16),
                   jax.ShapeDtypeStruct((N_PAIRS, s_rows, 2 * LANES), BF16),
                   jax.ShapeDtypeStruct((N_PAIRS, s_rows, LANES), BF16)],
        compiler_params=_cparams(("arbitrary",)),
        name="pre_mla",
    )(tok, mod, nw, wqa, qn, wqb, wkva, kvn, wkvb, cos, sin)


def _attn_tile(qn_ref, qp_ref, k_ref, v_ref, o_ref, s_len):
    tq = qn_ref.shape[1]
    lane = lax.broadcasted_iota(jnp.int32, (tq, LANES), 1)

    def pair(j, carry):
        qn = qn_ref[j]
        qp = qp_ref[j // 2]
        k = k_ref[j, :s_len, :]
        v = v_ref[j, :s_len, :]
        outs = []
        for e in range(2):
            qn_m = jnp.where(lane // QK_NOPE_DIM == e, qn, jnp.zeros_like(qn))
            qp_m = jnp.where(lane // QK_ROPE_DIM == 2 * (j % 2) + e, qp, jnp.zeros_like(qp))
            lhs = jnp.concatenate([qn_m, qp_m], axis=1)
            s = lax.dot_general(lhs, k, (((1,), (1,)), ((), ())), preferred_element_type=F32)
            p = jnp.exp2(s - jnp.max(s, axis=-1, keepdims=True))
            l = jnp.sum(p, axis=-1, keepdims=True)
            outs.append(_dot(p.astype(BF16), v) * (1.0 / l))
        o_ref[j] = jnp.where(lane < V_HEAD_DIM, outs[0], outs[1]).astype(BF16)
        return carry

    lax.fori_loop(0, N_PAIRS, pair, 0)


def _attn_kernel(qn_ref, qp_ref, k_ref, v_ref, o_ref, *, tpb, n_ctx):
    step = pl.program_id(1)

    @pl.when(step < tpb)
    def _():
        _attn_tile(qn_ref, qp_ref, k_ref, v_ref, o_ref, k_ref.shape[1])

    @pl.when(step >= tpb)
    def _():
        _attn_tile(qn_ref, qp_ref, k_ref, v_ref, o_ref, n_ctx)


def _attention(qn, qp, kcat, v, *, nb, tpb, cpb):
    t = qn.shape[1]
    tq = TOKEN_TILE
    nx = nb * tpb
    s_len = (tpb + cpb) * tq

    def qblk(b, i):
        return jnp.where(i < tpb, b * tpb + i, nx + b * cpb + (i - tpb))

    return pl.pallas_call(
        functools.partial(_attn_kernel, tpb=tpb, n_ctx=cpb * tq),
        grid=(nb, tpb + cpb),
        in_specs=[pl.BlockSpec((N_PAIRS, tq, LANES), lambda b, i: (0, qblk(b, i), 0)),
                  pl.BlockSpec((N_ROPE_GROUPS, tq, LANES), lambda b, i: (0, qblk(b, i), 0)),
                  pl.BlockSpec((N_PAIRS, s_len, 2 * LANES), lambda b, i: (0, b, 0)),
                  pl.BlockSpec((N_PAIRS, s_len, LANES), lambda b, i: (0, b, 0))],
        out_specs=pl.BlockSpec((N_PAIRS, tq, LANES), lambda b, i: (0, qblk(b, i), 0)),
        out_shape=jax.ShapeDtypeStruct((N_PAIRS, t, LANES), BF16),
        compiler_params=_cparams(("arbitrary", "arbitrary")),
        name="attention",
    )(qn, qp, kcat, v)


def _post_kernel(act_ref, wmix_ref, bmix_ref, xs_ref, mod_ref, nfw_ref, rwt_ref, rb_ref,
                 sw1_ref, sw3_ref, sw2_ref,
                 h2_out, base_out, idx_out, wt_out, rank_out, cnt_out, cnt_scr, *, pair_major):
    i = pl.program_id(0)
    tm = xs_ref.shape[0]

    @pl.when(i == 0)
    def _():
        cnt_scr[...] = jnp.zeros_like(cnt_scr)

    if pair_major:
        act = jnp.concatenate([act_ref[j] for j in range(N_PAIRS)], axis=1)
    else:
        act = act_ref[...]
    m = mod_ref[0]
    mix = _dot(act, wmix_ref[...]) + bmix_ref[...]
    xs = xs_ref[...] + m[2:3] * mix
    h2 = _rms(xs, nfw_ref[...]) * (1.0 + m[4:5]) + m[3:4]
    h2_out[...] = _pack_bf16_pairs(h2)

    h2b = h2.astype(BF16)
    hid = _silu(_dot(h2b, sw1_ref[...])) * _dot(h2b, sw3_ref[...])
    base_out[...] = xs + m[5:6] * _dot(hid.astype(BF16), sw2_ref[...])

    logits = lax.dot_general(rwt_ref[...], h2, (((1,), (1,)), ((), ())),
                             preferred_element_type=F32, precision=lax.Precision.HIGHEST)
    scores = _sigmoid(logits)
    sel = scores + rb_ref[...]
    per_group = N_EXPERTS // N_GROUPS
    neg = jnp.float32(-jnp.inf)
    sub_iota = lax.broadcasted_iota(jnp.int32, (per_group, tm), 0)
    gs_rows = []
    for g in range(N_GROUPS):
        sg = sel[g * per_group:(g + 1) * per_group, :]
        m1 = jnp.max(sg, axis=0, keepdims=True)
        first = jnp.min(jnp.where(sg == m1, sub_iota, per_group), axis=0, keepdims=True)
        m2 = jnp.max(jnp.where(sub_iota == first, neg, sg), axis=0, keepdims=True)
        gs_rows.append(m1 + m2)
    gs = jnp.concatenate(gs_rows, axis=0)
    g_iota = lax.broadcasted_iota(jnp.int32, (N_GROUPS, tm), 0)
    gmask = jnp.zeros((N_GROUPS, tm), F32)
    work = gs
    for _ in range(TOPK_GROUPS):
        mx = jnp.max(work, axis=0, keepdims=True)
        gi = jnp.min(jnp.where(work == mx, g_iota, N_GROUPS), axis=0, keepdims=True)
        pick = g_iota == gi
        gmask = jnp.where(pick, 1.0, gmask)
        work = jnp.where(pick, neg, work)
    emask = jnp.concatenate(
        [jnp.broadcast_to(gmask[g:g + 1, :], (per_group, tm)) for g in range(N_GROUPS)], axis=0)
    masked = jnp.where(emask > 0.5, sel, neg)
    e_iota = lax.broadcasted_iota(jnp.int32, (N_EXPERTS, tm), 0)
    picks, idx_rows, w_rows = [], [], []
    for _ in range(TOP_K):
        mx = jnp.max(masked, axis=0, keepdims=True)
        ei = jnp.min(jnp.where(masked == mx, e_iota, N_EXPERTS), axis=0, keepdims=True)
        pick = e_iota == ei
        picks.append(pick)
        idx_rows.append(ei)
        w_rows.append(jnp.sum(jnp.where(pick, scores, 0.0), axis=0, keepdims=True))
        masked = jnp.where(pick, neg, masked)
    wsum = w_rows[0]
    for r in w_rows[1:]:
        wsum = wsum + r
    wnorm = ROUTED_SCALE / wsum

    member = jnp.zeros((N_EXPERTS, tm), F32)
    for pick in picks:
        member = jnp.where(pick, 1.0, member)
    before = (lax.broadcasted_iota(jnp.int32, (tm, tm), 0)
              < lax.broadcasted_iota(jnp.int32, (tm, tm), 1))
    prefix = _dot(member.astype(BF16), jnp.where(before, 1.0, 0.0).astype(BF16))
    cnt = cnt_scr[...]
    rank_full = cnt[:, 0:1] + prefix
    rank_rows = [jnp.sum(jnp.where(pick, rank_full, 0.0), axis=0, keepdims=True)
                 for pick in picks]
    cnt = cnt + jnp.sum(member, axis=1, keepdims=True)
    cnt_scr[...] = cnt
    cnt_out[...] = cnt

    pad = SUBLANES - TOP_K
    zi = jnp.zeros((pad, tm), jnp.int32)
    zf = jnp.zeros((pad, tm), F32)
    idx_out[...] = jnp.concatenate(idx_rows + [zi], axis=0)
    wt_out[...] = jnp.concatenate([r * wnorm for r in w_rows] + [zf], axis=0)
    rank_out[...] = jnp.concatenate([r.astype(jnp.int32) for r in rank_rows] + [zi], axis=0)


def _post(act, wmix, bmix, xs, mod, nfw, rwt, rb, sw1, sw3, sw2, *, nt, mrow, pair_major):
    d = xs.shape[1]
    tm = TOKEN_TILE
    t = nt * tm
    full = lambda a: pl.BlockSpec(a.shape, lambda i: (0,) * a.ndim)
    if pair_major:
        act_spec = pl.BlockSpec((N_PAIRS, tm, LANES), lambda i: (0, i, 0))
    else:
        act_spec = pl.BlockSpec((tm, act.shape[1]), lambda i: (i, 0))
    row8 = pl.BlockSpec((SUBLANES, tm), lambda i: (0, i))
    return pl.pallas_call(
        functools.partial(_post_kernel, pair_major=pair_major),
        grid=(nt,),
        in_specs=[act_spec, full(wmix), full(bmix),
                  pl.BlockSpec((tm, d), lambda i: (i, 0)),
                  pl.BlockSpec((1, 6, d), lambda i: (mrow(i), 0, 0)),
                  full(nfw), full(rwt), full(rb), full(sw1), full(sw3), full(sw2)],
        out_specs=[pl.BlockSpec((tm, d // 2), lambda i: (i, 0)),
                   pl.BlockSpec((tm, d), lambda i: (i, 0)),
                   row8, row8, row8,
                   pl.BlockSpec((N_EXPERTS, LANES), lambda i: (0, 0))],
        out_shape=[jax.ShapeDtypeStruct((t, d // 2), jnp.uint32),
                   jax.ShapeDtypeStruct((t, d), F32),
                   jax.ShapeDtypeStruct((SUBLANES, t), jnp.int32),
                   jax.ShapeDtypeStruct((SUBLANES, t), F32),
                   jax.ShapeDtypeStruct((SUBLANES, t), jnp.int32),
                   jax.ShapeDtypeStruct((N_EXPERTS, LANES), F32)],
        scratch_shapes=[pltpu.VMEM((N_EXPERTS, LANES), F32)],
        compiler_params=_cparams(("arbitrary",)),
        name="post_mixer",
    )(act, wmix, bmix, xs, mod, nfw, rwt, rb, sw1, sw3, sw2)


def _dispatch_kernel(dest_ref, fill_ref, h2_ref, xg_out, zbuf, sem, zsem, *, t_total, tile):
    step = pl.program_id(0)
    base = step * tile
    rows = zbuf.shape[0]
    n_blocks = xg_out.shape[0] // rows

    @pl.when(step == 0)
    def _():
        zbuf[...] = jnp.zeros_like(zbuf)

        def fill_copy(n):
            return pltpu.make_async_copy(
                zbuf, xg_out.at[pl.ds(pl.multiple_of(n * rows, rows), rows)], zsem)

        def start(n, carry):
            @pl.when(fill_ref[n] > 0)
            def _():
                fill_copy(n).start()
            return carry

        def wait(n, carry):
            @pl.when(fill_ref[n] > 0)
            def _():
                fill_copy(n).wait()
            return carry

        lax.fori_loop(0, n_blocks, start, 0)
        lax.fori_loop(0, n_blocks, wait, 0)

    def row_copy(r, slot):
        return pltpu.make_async_copy(h2_ref.at[pl.ds(r, 1)], xg_out.at[pl.ds(slot, 1)], sem)

    def issue(r, carry):
        for k in range(TOP_K):
            row_copy(r, dest_ref[k * t_total + base + r]).start()
        return carry

    lax.fori_loop(0, tile, issue, 0, unroll=ISSUE_UNROLL)

    def drain(r, carry):
        for k in range(TOP_K):
            row_copy(0, 0).wait()
        return carry

    lax.fori_loop(0, tile, drain, 0, unroll=ISSUE_UNROLL)


def _dispatch(dest, fill, h2, n_slots):
    t, d = h2.shape
    tile = DISPATCH_TILE
    return pl.pallas_call(
        functools.partial(_dispatch_kernel, t_total=t, tile=tile),
        grid_spec=pltpu.PrefetchScalarGridSpec(
            num_scalar_prefetch=2,
            grid=(t // tile,),
            in_specs=[pl.BlockSpec((tile, d), lambda i, dr, fr: (i, 0))],
            out_specs=pl.BlockSpec(memory_space=pl.ANY),
            scratch_shapes=[pltpu.VMEM((EXPERT_ROWS, d), h2.dtype),
                            pltpu.SemaphoreType.DMA(()), pltpu.SemaphoreType.DMA(())]),
        out_shape=jax.ShapeDtypeStruct((n_slots, d), h2.dtype),
        compiler_params=_cparams(("arbitrary",)),
        name="moe_dispatch",
    )(dest, fill, h2)


def _expert_kernel(be_ref, nu_ref, x_ref, w1_ref, w3_ref, w2_ref, y_ref):
    del be_ref
    used = pl.program_id(0) < nu_ref[0]

    @pl.when(used)
    def _():
        x = _unpack_bf16_pairs(x_ref[...]).astype(BF16)
        hid = _silu(_dot(x, w1_ref[0])) * _dot(x, w3_ref[0])
        y_ref[...] = _pack_bf16_pairs(_dot(hid.astype(BF16), w2_ref[0]))

    @pl.when(jnp.logical_not(used))
    def _():
        y_ref[...] = jnp.zeros_like(y_ref)


def _experts(blk_e, n_used, xg, w1, w3, w2):
    p, dh = xg.shape
    rows = EXPERT_ROWS
    nblk = p // rows
    _, d, e_dim = w1.shape

    def xmap(n, be, nu):
        return (jnp.minimum(n, nu[0] - 1), 0)

    def wmap(n, be, nu):
        return (be[n], 0, 0)

    return pl.pallas_call(
        _expert_kernel,
        grid_spec=pltpu.PrefetchScalarGridSpec(
            num_scalar_prefetch=2,
            grid=(nblk,),
            in_specs=[pl.BlockSpec((rows, dh), xmap),
                      pl.BlockSpec((1, d, e_dim), wmap),
                      pl.BlockSpec((1, d, e_dim), wmap),
                      pl.BlockSpec((1, e_dim, d), wmap)],
            out_specs=pl.BlockSpec((rows, dh), lambda n, be, nu: (n, 0))),
        out_shape=jax.ShapeDtypeStruct((p, dh), jnp.uint32),
        compiler_params=_cparams(("arbitrary",)),
        name="moe_experts",
    )(blk_e, n_used, xg, w1, w3, w2)


def _combine_kernel(dest_ref, y_hbm, base_ref, wt_ref, mod_ref, fnw_ref, o_ref, buf, sem,
                    *, t_total, final_norm):
    tm = base_ref.shape[0]
    base = pl.program_id(0) * tm

    def row_copy(slot, k, r):
        return pltpu.make_async_copy(y_hbm.at[pl.ds(slot, 1)], buf.at[k, pl.ds(r, 1)], sem)

    def issue(r, carry):
        for k in range(TOP_K):
            row_copy(dest_ref[k * t_total + base + r], k, r).start()
        return carry

    lax.fori_loop(0, tm, issue, 0, unroll=ISSUE_UNROLL)

    def drain(r, carry):
        for k in range(TOP_K):
            row_copy(0, k, 0).wait()
        return carry

    lax.fori_loop(0, tm, drain, 0, unroll=ISSUE_UNROLL)

    wt = wt_ref[...]
    acc = wt[:, 0:1] * _unpack_bf16_pairs(buf[0])
    for k in range(1, TOP_K):
        acc = acc + wt[:, k:k + 1] * _unpack_bf16_pairs(buf[k])
    out = base_ref[...] + mod_ref[0][5:6] * acc
    if final_norm:
        out = _rms(out, fnw_ref[...])
    o_ref[...] = out


def _combine(dest, yg, base, wt_t, mod, fnw, *, nt, mrow, final_norm):
    d = base.shape[1]
    tm = COMBINE_TILE
    t = nt * tm
    return pl.pallas_call(
        functools.partial(_combine_kernel, t_total=base.shape[0], final_norm=final_norm),
        grid_spec=pltpu.PrefetchScalarGridSpec(
            num_scalar_prefetch=1,
            grid=(nt,),
            in_specs=[pl.BlockSpec(memory_space=pl.ANY),
                      pl.BlockSpec((tm, d), lambda i, dr: (i, 0)),
                      pl.BlockSpec((tm, SUBLANES), lambda i, dr: (i, 0)),
                      pl.BlockSpec((1, 6, d), lambda i, dr: (mrow(i), 0, 0)),
                      pl.BlockSpec((1, d), lambda i, dr: (0, 0))],
            out_specs=pl.BlockSpec((tm, d), lambda i, dr: (i, 0)),
            scratch_shapes=[pltpu.VMEM((TOP_K, tm, d // 2), jnp.uint32),
                            pltpu.SemaphoreType.DMA(())]),
        out_shape=jax.ShapeDtypeStruct((t, d), F32),
        compiler_params=_cparams(("arbitrary",)),
        name="moe_combine",
    )(dest, yg, base, wt_t, mod, fnw)


def _moe(h2, base, idx8, wt8, rank8, cnt, mod, fnw, w1, w3, w2, *, nt_out, mrow, final_norm):
    t, d = h2.shape
    rows = EXPERT_ROWS
    counts = cnt[:, 0].astype(jnp.int32)
    padded = (counts + rows - 1) // rows * rows
    pad_end = jnp.cumsum(padded)
    pad_start = pad_end - padded
    nblk = (t * TOP_K + rows - 1) // rows + N_EXPERTS
    e_ids = jnp.arange(N_EXPERTS, dtype=jnp.int32)
    idx = idx8[:TOP_K]
    start_of = jnp.sum(jnp.where(idx[:, :, None] == e_ids, pad_start, 0), axis=-1)
    dest = (start_of + rank8[:TOP_K]).reshape(-1)
    n_used = (pad_end[-1] // rows).astype(jnp.int32)
    blk_start = jnp.arange(nblk, dtype=jnp.int32) * rows
    e_of_blk = jnp.sum((pad_end[None, :] <= blk_start[:, None]).astype(jnp.int32), axis=1)
    valid_end = jnp.sum(jnp.where(e_of_blk[:, None] == e_ids, pad_start + counts, 0), axis=1)
    fill = (blk_start + rows > valid_end).astype(jnp.int32)
    e_last = jnp.max(jnp.where(counts > 0, e_ids, 0))
    blk_e = jnp.minimum(e_of_blk, e_last).astype(jnp.int32)
    xg = _dispatch(dest, fill, h2, nblk * rows)
    yg = _experts(blk_e, n_used.reshape(1), xg, w1, w3, w2)
    tile_ratio = TOKEN_TILE // COMBINE_TILE
    return _combine(dest, yg, base, wt8.T, mod, fnw, nt=nt_out * tile_ratio,
                    mrow=lambda i: mrow(i // tile_ratio), final_norm=final_norm)


def _pre_conv_kernel(xs_ref, mod_ref, nw_ref, w_ref, b_ref, o_ref):
    m = mod_ref[0]
    h = _rms(xs_ref[...], nw_ref[...]) * (1.0 + m[1:2]) + m[0:1]
    u = _dot(h.astype(BF16), w_ref[...]) + b_ref[...]
    d = o_ref.shape[1]
    o_ref[...] = u[:, :d] * _sigmoid(u[:, d:])


def _pre_conv(xs, mod, nw, w, b, *, nt, mrow):
    d = xs.shape[1]
    tm = TOKEN_TILE
    full = lambda a: pl.BlockSpec(a.shape, lambda i: (0,) * a.ndim)
    return pl.pallas_call(
        _pre_conv_kernel,
        grid=(nt,),
        in_specs=[pl.BlockSpec((tm, d), lambda i: (i, 0)),
                  pl.BlockSpec((1, 6, d), lambda i: (mrow(i), 0, 0)),
                  full(nw), full(w), full(b)],
        out_specs=pl.BlockSpec((tm, d), lambda i: (i, 0)),
        out_shape=jax.ShapeDtypeStruct((nt * tm, d), F32),
        compiler_params=_cparams(("arbitrary",)),
        name="pre_conv",
    )(xs, mod, nw, w, b)


def _conv_kernel(prev_ref, cur_ref, next_ref, dw_ref, db_ref, lw_ref, lb_ref, o_ref, win, conv,
                 *, tpb):
    tm, d = cur_ref.shape
    r = pl.program_id(0) % tpb
    halo = CONV_HALO
    pad = CONV_WIDTH // 2
    win[0:halo, :] = jnp.where(r > 0, prev_ref[...], 0.0)
    win[halo:halo + tm, :] = cur_ref[...]
    win[halo + tm:halo + tm + halo, :] = jnp.where(r < tpb - 1, next_ref[...], 0.0)

    rows = 64
    span = 2 * halo - SUBLANES
    for c in range(d // LANES):
        cs = slice(c * LANES, (c + 1) * LANES)
        taps = dw_ref[:, cs]
        bias = db_ref[:, cs]

        def chunk(q, carry):
            r0 = pl.multiple_of(q * rows, rows)
            slab = win[pl.ds(r0, rows + 2 * halo), cs]
            acc = jnp.zeros((rows, LANES), F32) + bias
            for b in range(SUBLANES):
                shifted = slab[b:b + rows + span, :]
                for j in range(CONV_WIDTH):
                    off = halo - pad + j
                    if off % SUBLANES == b:
                        a = off - b
                        acc = acc + taps[j:j + 1, :] * shifted[a:a + rows, :]
            conv[pl.ds(r0, rows), cs] = acc
            return carry

        lax.fori_loop(0, tm // rows, chunk, 0)

    u = conv[...]
    mu = jnp.mean(u, axis=-1, keepdims=True)
    var = jnp.mean(jnp.square(u - mu), axis=-1, keepdims=True)
    y = (u - mu) * lax.rsqrt(var + NORM_EPS) * lw_ref[...] + lb_ref[...]
    o_ref[...] = _silu(y).astype(BF16)


def _conv(glu, dw, db, lw, lb, *, nt, tpb):
    d = glu.shape[1]
    tm = TOKEN_TILE
    hb = tm // CONV_HALO
    n_halo_blocks = glu.shape[0] // CONV_HALO
    full = lambda a: pl.BlockSpec(a.shape, lambda i: (0,) * a.ndim)
    return pl.pallas_call(
        functools.partial(_conv_kernel, tpb=tpb),
        grid=(nt,),
        in_specs=[pl.BlockSpec((CONV_HALO, d), lambda i: (jnp.maximum(i * hb - 1, 0), 0)),
                  pl.BlockSpec((tm, d), lambda i: (i, 0)),
                  pl.BlockSpec((CONV_HALO, d),
                               lambda i: (jnp.minimum((i + 1) * hb, n_halo_blocks - 1), 0)),
                  full(dw), full(db), full(lw), full(lb)],
        out_specs=pl.BlockSpec((tm, d), lambda i: (i, 0)),
        out_shape=jax.ShapeDtypeStruct((nt * tm, d), BF16),
        scratch_shapes=[pltpu.VMEM((tm + 2 * CONV_HALO, d), F32), pltpu.VMEM((tm, d), F32)],
        compiler_params=_cparams(("arbitrary",)),
        name="dw_conv",
    )(glu, glu, glu, dw, db, lw, lb)


def _rope_tables(n_tokens, extra_rows):
    rows = n_tokens // GRID_W
    row = jnp.repeat(jnp.arange(rows, dtype=F32), GRID_W)
    col = jnp.tile(jnp.arange(GRID_W, dtype=F32), rows)
    axis_dim = QK_ROPE_DIM // 2
    inv_freq = ROPE_THETA ** (-jnp.arange(0, axis_dim, 2, dtype=F32) / axis_dim)
    ang_r = row[:, None] * inv_freq
    ang_c = col[:, None] * inv_freq
    ang = jnp.concatenate([ang_r, ang_r, ang_c, ang_c], axis=-1)
    cos = jnp.concatenate([jnp.cos(ang), jnp.ones((extra_rows, QK_ROPE_DIM), F32)], axis=0)
    sin = jnp.concatenate([jnp.sin(ang), jnp.zeros((extra_rows, QK_ROPE_DIM), F32)], axis=0)
    return jnp.tile(cos, (1, N_HEADS)), jnp.tile(sin, (1, N_HEADS))


def kernel(x, c, ctx, c_ctx, mod_w, mod_b, norm_mix_w, norm_ffn_w, mla_wq_a, mla_q_norm, mla_wq_b, mla_wkv_a, mla_kv_norm, mla_wkv_b, mla_wo, conv_pw1_w, conv_pw1_b, conv_dw_w, conv_dw_b, conv_norm_w, conv_norm_b, conv_pw2_w, conv_pw2_b, router_w, router_bias, exp_w1, exp_w3, exp_w2, shared_w1, shared_w3, shared_w2, final_norm_w):
    nb, seq, d = x.shape
    n_ctx = ctx.shape[1]
    tm = TOKEN_TILE
    tpb, cpb = seq // tm, n_ctx // tm
    nx, nc = nb * tpb, nb * cpb
    row = lambda a: a.reshape(1, -1)

    pad_rows = (-(nb + 1)) % SUBLANES
    cvec = jnp.concatenate([c, c_ctx[None, :], jnp.zeros((pad_rows, d), F32)], axis=0)
    mod = _modulation(cvec, mod_w, mod_b).reshape(mod_w.shape[0], cvec.shape[0], 6, d)

    def mrow_all(i):
        return jnp.where(i < nx, i // tpb, nb)

    tok = jnp.concatenate([x.reshape(nb * seq, d), ctx.reshape(nb * n_ctx, d)], axis=0)
    wqb = mla_wq_b[0].reshape(Q_LORA_RANK, N_HEADS, QK_NOPE_DIM + QK_ROPE_DIM)
    wqb = jnp.concatenate([wqb[:, :, :QK_NOPE_DIM].reshape(Q_LORA_RANK, -1),
                           wqb[:, :, QK_NOPE_DIM:].reshape(Q_LORA_RANK, -1)], axis=1)
    wkva = mla_wkv_a[0]
    wkva = jnp.concatenate([wkva[:, :KV_LORA_RANK]]
                           + [wkva[:, KV_LORA_RANK:]] * (LANES // QK_ROPE_DIM), axis=1)
    wkvb = mla_wkv_b[0].reshape(KV_LORA_RANK, N_HEADS, QK_NOPE_DIM + V_HEAD_DIM)
    wkvb = jnp.concatenate([wkvb[:, :, :QK_NOPE_DIM].reshape(KV_LORA_RANK, -1),
                            wkvb[:, :, QK_NOPE_DIM:].reshape(KV_LORA_RANK, -1)], axis=1)
    cos, sin = _rope_tables(seq, tm)
    qn, qp, kcat, v = _pre_mla(
        tok, mod[0], row(norm_mix_w[0]), mla_wq_a[0].astype(BF16), row(mla_q_norm[0]),
        wqb.astype(BF16), wkva.astype(BF16), row(mla_kv_norm[0]), wkvb.astype(BF16), cos, sin,
        nb=nb, tpb=tpb, cpb=cpb)
    o = _attention(qn, qp, kcat, v, nb=nb, tpb=tpb, cpb=cpb)

    moe_w = lambda i: (exp_w1[i].astype(BF16), exp_w3[i].astype(BF16), exp_w2[i].astype(BF16))
    post_w = lambda i: (row(norm_ffn_w[i]), router_w[i].T, router_bias[i].reshape(-1, 1),
                        shared_w1[i].astype(BF16), shared_w3[i].astype(BF16),
                        shared_w2[i].astype(BF16))
    h2, base, idx8, wt8, rank8, cnt = _post(
        o, mla_wo[0].astype(BF16), jnp.zeros((1, d), F32), tok, mod[0], *post_w(0),
        nt=nx + nc, mrow=mrow_all, pair_major=True)
    xs = _moe(h2, base, idx8, wt8, rank8, cnt, mod[0], row(final_norm_w), *moe_w(0),
              nt_out=nx + nc, mrow=mrow_all, final_norm=False)

    mrow_x = lambda i: i // tpb
    glu = _pre_conv(xs, mod[1], row(norm_mix_w[1]), conv_pw1_w[0].astype(BF16),
                    row(conv_pw1_b[0]), nt=nx, mrow=mrow_x)
    taps = jnp.concatenate([conv_dw_w[0], jnp.zeros((1, d), F32)], axis=0)
    act = _conv(glu, taps, row(conv_dw_b[0]), row(conv_norm_w[0]), row(conv_norm_b[0]),
                nt=nx, tpb=tpb)
    h2, base, idx8, wt8, rank8, cnt = _post(
        act, conv_pw2_w[0].astype(BF16), row(conv_pw2_b[0]), xs, mod[1],
        *post_w(1), nt=nx, mrow=mrow_x, pair_major=False)
    out = _moe(h2, base, idx8, wt8, rank8, cnt, mod[1], row(final_norm_w), *moe_w(1),
               nt_out=nx, mrow=mrow_x, final_norm=True)
    return out.reshape(nb, seq, d)
```

```python
import functools

import jax
import jax.numpy as jnp
from jax import lax
from jax.experimental import pallas as pl
from jax.experimental.pallas import tpu as pltpu

N_HEADS = 16
QK_NOPE_DIM = 64
QK_ROPE_DIM = 32
V_HEAD_DIM = 64
Q_LORA_RANK = 512
KV_LORA_RANK = 256
ROPE_THETA = 10000.0
GRID_W = 64
CONV_WIDTH = 31
N_EXPERTS = 64
TOP_K = 6
N_GROUPS = 8
TOPK_GROUPS = 4
ROUTED_SCALE = 2.5
NORM_EPS = 1e-6
LOG2_E = 1.4426950408889634

LANES = 128
SUBLANES = 8
TOKEN_TILE = 256
POST_TILE = 512
EXPERT_ROWS = 256
DISPATCH_TILE = 256
COMBINE_TILE = 128
ISSUE_UNROLL = 4
CONV_HALO = 16
N_PAIRS = N_HEADS // 2
N_ROPE_GROUPS = N_HEADS // 4
VMEM_LIMIT = 56 * 1024 * 1024

F32 = jnp.float32
BF16 = jnp.bfloat16


def _cparams(sem):
    return pltpu.CompilerParams(dimension_semantics=sem, vmem_limit_bytes=VMEM_LIMIT)


def _rms(x, w):
    return x * lax.rsqrt(jnp.mean(x * x, axis=-1, keepdims=True) + NORM_EPS) * w


def _sigmoid(x):
    return 1.0 / (1.0 + jnp.exp(-x))


def _silu(x):
    return x * _sigmoid(x)


def _dot(a, b):
    return jnp.dot(a, b, preferred_element_type=F32)


def _pack_bf16_pairs(x):
    n = x.shape[1] // 2
    hi = lax.bitcast_convert_type(x[:, :n].astype(BF16).astype(F32), jnp.uint32)
    lo = lax.bitcast_convert_type(x[:, n:].astype(BF16).astype(F32), jnp.uint32)
    return hi | (lo >> 16)


def _unpack_bf16_pairs(p):
    hi = lax.bitcast_convert_type(p & jnp.uint32(0xFFFF0000), F32)
    lo = lax.bitcast_convert_type(p << 16, F32)
    return jnp.concatenate([hi, lo], axis=1)


def _store_row_groups(ref, packed):
    m, n = packed.shape
    g = n // LANES
    for c in range(g):
        ref[pl.ds(c, m, stride=g), :] = packed[:, c * LANES:(c + 1) * LANES]


def _load_row_groups(ref, m, g, start=0):
    return jnp.concatenate([ref[pl.ds(start + c, m, stride=g), :] for c in range(g)], axis=1)


def _mod_kernel(c_ref, w_ref, b_ref, o_ref):
    c = c_ref[...]
    o_ref[0] = jnp.dot(_silu(c), w_ref[0], preferred_element_type=F32,
                       precision=lax.Precision.HIGHEST) + b_ref[0]


def _modulation(cvec, mod_w, mod_b):
    depth, d, n = mod_w.shape
    rows = cvec.shape[0]
    tn = 1536
    return pl.pallas_call(
        _mod_kernel,
        grid=(depth, n // tn),
        in_specs=[pl.BlockSpec((rows, d), lambda l, j: (0, 0)),
                  pl.BlockSpec((1, d, tn), lambda l, j: (l, 0, j)),
                  pl.BlockSpec((1, 1, tn), lambda l, j: (l, 0, j))],
        out_specs=pl.BlockSpec((1, rows, tn), lambda l, j: (l, 0, j)),
        out_shape=jax.ShapeDtypeStruct((depth, rows, n), F32),
        compiler_params=_cparams(("arbitrary", "arbitrary")),
        name="modulation",
    )(cvec, mod_w, mod_b.reshape(depth, 1, n))


def _rope(v, cos, sin):
    n = v.shape[1]
    lane = lax.broadcasted_iota(jnp.int32, v.shape, 1)
    up = pltpu.roll(v, 8, axis=1)
    dn = pltpu.roll(v, n - 8, axis=1)
    rot = jnp.where(lane % 16 < 8, -dn, up)
    return v * cos + rot * sin


def _pre_mla_kernel(tok_ref, mod_ref, nw_ref, wqa_ref, qn_ref, wqb_ref, wkva_ref, kvn_ref,
                    wkvb_ref, cos_ref, sin_ref, qn_out, qp_out, k_out, v_out):
    x = tok_ref[...]
    m = mod_ref[0]
    h = _rms(x, nw_ref[...]) * (1.0 + m[1:2]) + m[0:1]
    hb = h.astype(BF16)
    cos = cos_ref[...]
    sin = sin_ref[...]

    qa = _rms(_dot(hb, wqa_ref[...]), qn_ref[...])
    scale = (QK_NOPE_DIM + QK_ROPE_DIM) ** -0.5 * LOG2_E
    q = _dot(qa.astype(BF16), wqb_ref[...]) * scale
    d_nope = N_HEADS * QK_NOPE_DIM
    q_pe = _rope(q[:, d_nope:], cos, sin)

    kva = _dot(hb, wkva_ref[...])
    kpe4_t = _rope(kva[:, KV_LORA_RANK:], cos[:, :LANES], sin[:, :LANES]).T.astype(BF16)
    ckv = _rms(kva[:, :KV_LORA_RANK], kvn_ref[...])
    kv = _dot(ckv.astype(BF16), wkvb_ref[...])

    for j in range(N_PAIRS):
        sl = slice(j * LANES, (j + 1) * LANES)
        qn_out[j] = q[:, sl].astype(BF16)
        k_out[j, :LANES, :] = kv[:, sl].T.astype(BF16)
        k_out[j, LANES:, :] = kpe4_t
        v_out[j] = kv[:, d_nope + j * LANES:d_nope + (j + 1) * LANES].astype(BF16)
    for g in range(N_ROPE_GROUPS):
        qp_out[g] = q_pe[:, g * LANES:(g + 1) * LANES].astype(BF16)


def _pre_mla(tok, mod, nw, wqa, qn, wqb, wkva, kvn, wkvb, cos, sin, *, nb, tpb, cpb):
    t, d = tok.shape
    tm = TOKEN_TILE
    nx = nb * tpb
    spb = tpb + cpb

    def mrow(i):
        return jnp.where(i < nx, i // tpb, nb)

    def kvblk(i):
        ic = i - nx
        return jnp.where(i < nx, (i // tpb) * spb + cpb + i % tpb, (ic // cpb) * spb + ic % cpb)

    def ropeblk(i):
        return jnp.where(i < nx, i % tpb, tpb)

    full = lambda a: pl.BlockSpec(a.shape, lambda i: (0,) * a.ndim)
    s_rows = nb * spb * tm
    return pl.pallas_call(
        _pre_mla_kernel,
        grid=(t // tm,),
        in_specs=[pl.BlockSpec((tm, d), lambda i: (i, 0)),
                  pl.BlockSpec((1, 6, d), lambda i: (mrow(i), 0, 0)),
                  full(nw), full(wqa), full(qn), full(wqb), full(wkva), full(kvn), full(wkvb),
                  pl.BlockSpec((tm, cos.shape[1]), lambda i: (ropeblk(i), 0)),
                  pl.BlockSpec((tm, sin.shape[1]), lambda i: (ropeblk(i), 0))],
        out_specs=[pl.BlockSpec((N_PAIRS, tm, LANES), lambda i: (0, i, 0)),
                   pl.BlockSpec((N_ROPE_GROUPS, tm, LANES), lambda i: (0, i, 0)),
                   pl.BlockSpec((N_PAIRS, 2 * LANES, tm), lambda i: (0, 0, kvblk(i))),
                   pl.BlockSpec((N_PAIRS, tm, LANES), lambda i: (0, kvblk(i), 0))],
        out_shape=[jax.ShapeDtypeStruct((N_PAIRS, t, LANES), BF16),
                   jax.ShapeDtypeStruct((N_ROPE_GROUPS, t, LANES), BF16),
                   jax.ShapeDtypeStruct((N_PAIRS, 2 * LANES, s_rows), BF16),
                   jax.ShapeDtypeStruct((N_PAIRS, s_rows, LANES), BF16)],
        compiler_params=_cparams(("arbitrary",)),
        name="pre_mla",
    )(tok, mod, nw, wqa, qn, wqb, wkva, kvn, wkvb, cos, sin)


def _attn_tile(qn_ref, qp_ref, k_ref, v_ref, o_ref, s_len):
    tq = qn_ref.shape[1]
    lane = lax.broadcasted_iota(jnp.int32, (tq, LANES), 1)

    def pair(j, carry):
        qn = qn_ref[j]
        qp = qp_ref[j // 2]
        k = k_ref[j, :, :s_len]
        v = v_ref[j, :s_len, :]
        outs = []
        for e in range(2):
            qn_m = jnp.where(lane // QK_NOPE_DIM == e, qn, jnp.zeros_like(qn))
            qp_m = jnp.where(lane // QK_ROPE_DIM == 2 * (j % 2) + e, qp, jnp.zeros_like(qp))
            lhs = jnp.concatenate([qn_m, qp_m], axis=1)
            s = _dot(lhs, k)
            p = jnp.exp2(s - jnp.max(s, axis=-1, keepdims=True))
            l = jnp.sum(p, axis=-1, keepdims=True)
            outs.append(_dot(p.astype(BF16), v) * (1.0 / l))
        o_ref[j] = jnp.where(lane < V_HEAD_DIM, outs[0], outs[1]).astype(BF16)
        return carry

    lax.fori_loop(0, N_PAIRS, pair, 0, unroll=4)


def _attn_kernel(qn_ref, qp_ref, k_ref, v_ref, o_ref, *, tpb, n_ctx):
    step = pl.program_id(1)

    @pl.when(step < tpb)
    def _():
        _attn_tile(qn_ref, qp_ref, k_ref, v_ref, o_ref, v_ref.shape[1])

    @pl.when(step >= tpb)
    def _():
        _attn_tile(qn_ref, qp_ref, k_ref, v_ref, o_ref, n_ctx)


def _attention(qn, qp, kcat, v, *, nb, tpb, cpb):
    t = qn.shape[1]
    tq = TOKEN_TILE
    nx = nb * tpb
    s_len = (tpb + cpb) * tq

    def qblk(b, i):
        return jnp.where(i < tpb, b * tpb + i, nx + b * cpb + (i - tpb))

    return pl.pallas_call(
        functools.partial(_attn_kernel, tpb=tpb, n_ctx=cpb * tq),
        grid=(nb, tpb + cpb),
        in_specs=[pl.BlockSpec((N_PAIRS, tq, LANES), lambda b, i: (0, qblk(b, i), 0)),
                  pl.BlockSpec((N_ROPE_GROUPS, tq, LANES), lambda b, i: (0, qblk(b, i), 0)),
                  pl.BlockSpec((N_PAIRS, 2 * LANES, s_len), lambda b, i: (0, 0, b)),
                  pl.BlockSpec((N_PAIRS, s_len, LANES), lambda b, i: (0, b, 0))],
        out_specs=pl.BlockSpec((N_PAIRS, tq, LANES), lambda b, i: (0, qblk(b, i), 0)),
        out_shape=jax.ShapeDtypeStruct((N_PAIRS, t, LANES), BF16),
        compiler_params=_cparams(("arbitrary", "arbitrary")),
        name="attention",
    )(qn, qp, kcat, v)


def _post_kernel(act_ref, wmix_ref, bmix_ref, xs_ref, mod_ref, nfw_ref, rwt_ref, rb_ref,
                 sw1_ref, sw3_ref, sw2_ref,
                 h2_out, base_out, idx_out, wt_out, rank_out, cnt_out, cnt_scr, *, pair_major):
    i = pl.program_id(0)
    tm = xs_ref.shape[0]

    @pl.when(i == 0)
    def _():
        cnt_scr[...] = jnp.zeros_like(cnt_scr)

    if pair_major:
        act = jnp.concatenate([act_ref[j] for j in range(N_PAIRS)], axis=1)
    else:
        act = act_ref[...]
    m = mod_ref[0]
    mix = _dot(act, wmix_ref[...]) + bmix_ref[...]
    xs = xs_ref[...] + m[2:3] * mix
    h2 = _rms(xs, nfw_ref[...]) * (1.0 + m[4:5]) + m[3:4]
    _store_row_groups(h2_out, _pack_bf16_pairs(h2))

    h2b = h2.astype(BF16)
    hid = _silu(_dot(h2b, sw1_ref[...])) * _dot(h2b, sw3_ref[...])
    base_out[...] = xs + m[5:6] * _dot(hid.astype(BF16), sw2_ref[...])

    logits = lax.dot_general(rwt_ref[...], h2, (((1,), (1,)), ((), ())),
                             preferred_element_type=F32, precision=lax.Precision.HIGHEST)
    scores = _sigmoid(logits)
    sel = scores + rb_ref[...]
    per_group = N_EXPERTS // N_GROUPS
    neg = jnp.float32(-jnp.inf)
    sub_iota = lax.broadcasted_iota(jnp.int32, (per_group, tm), 0)
    gs_rows = []
    for g in range(N_GROUPS):
        sg = sel[g * per_group:(g + 1) * per_group, :]
        m1 = jnp.max(sg, axis=0, keepdims=True)
        first = jnp.min(jnp.where(sg == m1, sub_iota, per_group), axis=0, keepdims=True)
        m2 = jnp.max(jnp.where(sub_iota == first, neg, sg), axis=0, keepdims=True)
        gs_rows.append(m1 + m2)
    gs = jnp.concatenate(gs_rows, axis=0)
    g_iota = lax.broadcasted_iota(jnp.int32, (N_GROUPS, tm), 0)
    gmask = jnp.zeros((N_GROUPS, tm), F32)
    work = gs
    for _ in range(TOPK_GROUPS):
        mx = jnp.max(work, axis=0, keepdims=True)
        gi = jnp.min(jnp.where(work == mx, g_iota, N_GROUPS), axis=0, keepdims=True)
        pick = g_iota == gi
        gmask = jnp.where(pick, 1.0, gmask)
        work = jnp.where(pick, neg, work)
    emask = jnp.concatenate(
        [jnp.broadcast_to(gmask[g:g + 1, :], (per_group, tm)) for g in range(N_GROUPS)], axis=0)
    masked = jnp.where(emask > 0.5, sel, neg)
    e_iota = lax.broadcasted_iota(jnp.int32, (N_EXPERTS, tm), 0)
    picks, idx_rows, w_rows = [], [], []
    for _ in range(TOP_K):
        mx = jnp.max(masked, axis=0, keepdims=True)
        ei = jnp.min(jnp.where(masked == mx, e_iota, N_EXPERTS), axis=0, keepdims=True)
        pick = e_iota == ei
        picks.append(pick)
        idx_rows.append(ei)
        w_rows.append(jnp.sum(jnp.where(pick, scores, 0.0), axis=0, keepdims=True))
        masked = jnp.where(pick, neg, masked)
    wsum = w_rows[0]
    for r in w_rows[1:]:
        wsum = wsum + r
    wnorm = ROUTED_SCALE / wsum

    member = jnp.zeros((N_EXPERTS, tm), F32)
    for pick in picks:
        member = jnp.where(pick, 1.0, member)
    before = (lax.broadcasted_iota(jnp.int32, (tm, tm), 0)
              < lax.broadcasted_iota(jnp.int32, (tm, tm), 1))
    prefix = _dot(member.astype(BF16), jnp.where(before, 1.0, 0.0).astype(BF16))
    cnt = cnt_scr[...]
    rank_full = cnt[:, 0:1] + prefix
    rank_rows = [jnp.sum(jnp.where(pick, rank_full, 0.0), axis=0, keepdims=True)
                 for pick in picks]
    cnt = cnt + jnp.sum(member, axis=1, keepdims=True)
    cnt_scr[...] = cnt
    cnt_out[...] = cnt

    pad = SUBLANES - TOP_K
    zi = jnp.zeros((pad, tm), jnp.int32)
    zf = jnp.zeros((pad, tm), F32)
    idx_out[...] = jnp.concatenate(idx_rows + [zi], axis=0)
    wt_out[...] = jnp.concatenate([r * wnorm for r in w_rows] + [zf], axis=0)
    rank_out[...] = jnp.concatenate([r.astype(jnp.int32) for r in rank_rows] + [zi], axis=0)


def _post(act, wmix, bmix, xs, mod, nfw, rwt, rb, sw1, sw3, sw2, *, tm, nt, mrow, pair_major):
    d = xs.shape[1]
    t = nt * tm
    g = d // 2 // LANES
    full = lambda a: pl.BlockSpec(a.shape, lambda i: (0,) * a.ndim)
    if pair_major:
        act_spec = pl.BlockSpec((N_PAIRS, tm, LANES), lambda i: (0, i, 0))
    else:
        act_spec = pl.BlockSpec((tm, act.shape[1]), lambda i: (i, 0))
    row8 = pl.BlockSpec((SUBLANES, tm), lambda i: (0, i))
    return pl.pallas_call(
        functools.partial(_post_kernel, pair_major=pair_major),
        grid=(nt,),
        in_specs=[act_spec, full(wmix), full(bmix),
                  pl.BlockSpec((tm, d), lambda i: (i, 0)),
                  pl.BlockSpec((1, 6, d), lambda i: (mrow(i), 0, 0)),
                  full(nfw), full(rwt), full(rb), full(sw1), full(sw3), full(sw2)],
        out_specs=[pl.BlockSpec((tm * g, LANES), lambda i: (i, 0)),
                   pl.BlockSpec((tm, d), lambda i: (i, 0)),
                   row8, row8, row8,
                   pl.BlockSpec((N_EXPERTS, LANES), lambda i: (0, 0))],
        out_shape=[jax.ShapeDtypeStruct((t * g, LANES), jnp.uint32),
                   jax.ShapeDtypeStruct((t, d), F32),
                   jax.ShapeDtypeStruct((SUBLANES, t), jnp.int32),
                   jax.ShapeDtypeStruct((SUBLANES, t), F32),
                   jax.ShapeDtypeStruct((SUBLANES, t), jnp.int32),
                   jax.ShapeDtypeStruct((N_EXPERTS, LANES), F32)],
        scratch_shapes=[pltpu.VMEM((N_EXPERTS, LANES), F32)],
        compiler_params=_cparams(("arbitrary",)),
        name="post_mixer",
    )(act, wmix, bmix, xs, mod, nfw, rwt, rb, sw1, sw3, sw2)


def _dispatch_kernel(dest_ref, fill_ref, h2_ref, xg_out, zbuf, sem, zsem, *, t_total, tile, g):
    step = pl.program_id(0)
    base = step * tile
    rows = zbuf.shape[0]
    n_blocks = xg_out.shape[0] // rows

    @pl.when(step == 0)
    def _():
        zbuf[...] = jnp.zeros_like(zbuf)

        def fill_copy(n):
            return pltpu.make_async_copy(
                zbuf, xg_out.at[pl.ds(pl.multiple_of(n * rows, rows), rows)], zsem)

        def start(n, carry):
            @pl.when(fill_ref[n] > 0)
            def _():
                fill_copy(n).start()
            return carry

        def wait(n, carry):
            @pl.when(fill_ref[n] > 0)
            def _():
                fill_copy(n).wait()
            return carry

        lax.fori_loop(0, n_blocks, start, 0)
        lax.fori_loop(0, n_blocks, wait, 0)

    def row_copy(r, slot):
        return pltpu.make_async_copy(h2_ref.at[pl.ds(pl.multiple_of(r * g, g), g)],
                                     xg_out.at[pl.ds(pl.multiple_of(slot * g, g), g)], sem)

    def issue(r, carry):
        for k in range(TOP_K):
            row_copy(r, dest_ref[k * t_total + base + r]).start(priority=k % 2)
        return carry

    lax.fori_loop(0, tile, issue, 0, unroll=ISSUE_UNROLL)

    def drain(r, carry):
        for k in range(TOP_K):
            row_copy(0, 0).wait()
        return carry

    lax.fori_loop(0, tile, drain, 0, unroll=ISSUE_UNROLL)


def _dispatch(dest, fill, h2, n_slots, g):
    t = h2.shape[0] // g
    tile = DISPATCH_TILE
    return pl.pallas_call(
        functools.partial(_dispatch_kernel, t_total=t, tile=tile, g=g),
        grid_spec=pltpu.PrefetchScalarGridSpec(
            num_scalar_prefetch=2,
            grid=(t // tile,),
            in_specs=[pl.BlockSpec((tile * g, LANES), lambda i, dr, fr: (i, 0))],
            out_specs=pl.BlockSpec(memory_space=pl.ANY),
            scratch_shapes=[pltpu.VMEM((EXPERT_ROWS * g, LANES), h2.dtype),
                            pltpu.SemaphoreType.DMA(()), pltpu.SemaphoreType.DMA(())]),
        out_shape=jax.ShapeDtypeStruct((n_slots * g, LANES), h2.dtype),
        compiler_params=_cparams(("arbitrary",)),
        name="moe_dispatch",
    )(dest, fill, h2)


def _expert_kernel(be_ref, nu_ref, x_ref, w1_ref, w3_ref, w2_ref, y_ref, w1b, w3b, w2b, *, g):
    n = pl.program_id(0)
    used = n < nu_ref[0]
    rows = x_ref.shape[0] // g

    @pl.when(jnp.logical_or(n == 0, be_ref[n] != be_ref[jnp.maximum(n - 1, 0)]))
    def _():
        w1b[...] = w1_ref[0].astype(BF16)
        w3b[...] = w3_ref[0].astype(BF16)
        w2b[...] = w2_ref[0].astype(BF16)

    @pl.when(used)
    def _():
        x = _unpack_bf16_pairs(_load_row_groups(x_ref, rows, g)).astype(BF16)
        hid = _silu(_dot(x, w1b[...])) * _dot(x, w3b[...])
        _store_row_groups(y_ref, _pack_bf16_pairs(_dot(hid.astype(BF16), w2b[...])))

    @pl.when(jnp.logical_not(used))
    def _():
        y_ref[...] = jnp.zeros_like(y_ref)


def _experts(blk_e, n_used, xg, w1, w3, w2, g):
    rows = EXPERT_ROWS * g
    nblk = xg.shape[0] // rows
    _, d, e_dim = w1.shape

    def xmap(n, be, nu):
        return (jnp.minimum(n, nu[0] - 1), 0)

    def wmap(n, be, nu):
        return (be[n], 0, 0)

    return pl.pallas_call(
        functools.partial(_expert_kernel, g=g),
        grid_spec=pltpu.PrefetchScalarGridSpec(
            num_scalar_prefetch=2,
            grid=(nblk,),
            in_specs=[pl.BlockSpec((rows, LANES), xmap),
                      pl.BlockSpec((1, d, e_dim), wmap),
                      pl.BlockSpec((1, d, e_dim), wmap),
                      pl.BlockSpec((1, e_dim, d), wmap)],
            out_specs=pl.BlockSpec((rows, LANES), lambda n, be, nu: (n, 0)),
            scratch_shapes=[pltpu.VMEM((d, e_dim), BF16), pltpu.VMEM((d, e_dim), BF16),
                            pltpu.VMEM((e_dim, d), BF16)]),
        out_shape=jax.ShapeDtypeStruct(xg.shape, jnp.uint32),
        compiler_params=_cparams(("arbitrary",)),
        name="moe_experts",
    )(blk_e, n_used, xg, w1, w3, w2)


def _combine_kernel(dest_ref, y_hbm, base_ref, wt_ref, mod_ref, fnw_ref, o_ref, buf, sem,
                    *, t_total, final_norm, g):
    tm = base_ref.shape[0]
    base = pl.program_id(0) * tm

    def row_copy(slot, k, r):
        return pltpu.make_async_copy(y_hbm.at[pl.ds(pl.multiple_of(slot * g, g), g)],
                                     buf.at[pl.ds(pl.multiple_of((k * tm + r) * g, g), g)], sem)

    def issue(r, carry):
        for k in range(TOP_K):
            row_copy(dest_ref[k * t_total + base + r], k, r).start(priority=k % 2)
        return carry

    lax.fori_loop(0, tm, issue, 0, unroll=ISSUE_UNROLL)

    def drain(r, carry):
        for k in range(TOP_K):
            row_copy(0, k, 0).wait()
        return carry

    lax.fori_loop(0, tm, drain, 0, unroll=ISSUE_UNROLL)

    wt = wt_ref[...]
    acc = wt[:, 0:1] * _unpack_bf16_pairs(_load_row_groups(buf, tm, g))
    for k in range(1, TOP_K):
        rows_k = _load_row_groups(buf, tm, g, start=k * tm * g)
        acc = acc + wt[:, k:k + 1] * _unpack_bf16_pairs(rows_k)
    out = base_ref[...] + mod_ref[0][5:6] * acc
    if final_norm:
        out = _rms(out, fnw_ref[...])
    o_ref[...] = out


def _combine(dest, yg, base, wt_t, mod, fnw, *, nt, mrow, final_norm, g):
    d = base.shape[1]
    tm = COMBINE_TILE
    t = nt * tm
    return pl.pallas_call(
        functools.partial(_combine_kernel, t_total=base.shape[0], final_norm=final_norm, g=g),
        grid_spec=pltpu.PrefetchScalarGridSpec(
            num_scalar_prefetch=1,
            grid=(nt,),
            in_specs=[pl.BlockSpec(memory_space=pl.ANY),
                      pl.BlockSpec((tm, d), lambda i, dr: (i, 0)),
                      pl.BlockSpec((tm, SUBLANES), lambda i, dr: (i, 0)),
                      pl.BlockSpec((1, 6, d), lambda i, dr: (mrow(i), 0, 0)),
                      pl.BlockSpec((1, d), lambda i, dr: (0, 0))],
            out_specs=pl.BlockSpec((tm, d), lambda i, dr: (i, 0)),
            scratch_shapes=[pltpu.VMEM((TOP_K * tm * g, LANES), jnp.uint32),
                            pltpu.SemaphoreType.DMA(())]),
        out_shape=jax.ShapeDtypeStruct((t, d), F32),
        compiler_params=_cparams(("arbitrary",)),
        name="moe_combine",
    )(dest, yg, base, wt_t, mod, fnw)


def _moe(h2, base, idx8, wt8, rank8, cnt, mod, fnw, w1, w3, w2, *, nt_out, mrow, final_norm):
    g = base.shape[1] // 2 // LANES
    t = h2.shape[0] // g
    rows = EXPERT_ROWS
    counts = cnt[:, 0].astype(jnp.int32)
    padded = (counts + rows - 1) // rows * rows
    pad_end = jnp.cumsum(padded)
    pad_start = pad_end - padded
    nblk = (t * TOP_K + rows - 1) // rows + N_EXPERTS
    e_ids = jnp.arange(N_EXPERTS, dtype=jnp.int32)
    idx = idx8[:TOP_K]
    start_of = jnp.sum(jnp.where(idx[:, :, None] == e_ids, pad_start, 0), axis=-1)
    dest = (start_of + rank8[:TOP_K]).reshape(-1)
    n_used = (pad_end[-1] // rows).astype(jnp.int32)
    blk_start = jnp.arange(nblk, dtype=jnp.int32) * rows
    e_of_blk = jnp.sum((pad_end[None, :] <= blk_start[:, None]).astype(jnp.int32), axis=1)
    valid_end = jnp.sum(jnp.where(e_of_blk[:, None] == e_ids, pad_start + counts, 0), axis=1)
    fill = (blk_start + rows > valid_end).astype(jnp.int32)
    e_last = jnp.max(jnp.where(counts > 0, e_ids, 0))
    blk_e = jnp.minimum(e_of_blk, e_last).astype(jnp.int32)
    xg = _dispatch(dest, fill, h2, nblk * rows, g)
    yg = _experts(blk_e, n_used.reshape(1), xg, w1, w3, w2, g)
    tile_ratio = TOKEN_TILE // COMBINE_TILE
    return _combine(dest, yg, base, wt8.T, mod, fnw, nt=nt_out * tile_ratio,
                    mrow=lambda i: mrow(i // tile_ratio), final_norm=final_norm, g=g)


def _pre_conv_kernel(xs_ref, mod_ref, nw_ref, w_ref, b_ref, o_ref):
    m = mod_ref[0]
    h = _rms(xs_ref[...], nw_ref[...]) * (1.0 + m[1:2]) + m[0:1]
    u = _dot(h.astype(BF16), w_ref[...]) + b_ref[...]
    d = o_ref.shape[1]
    o_ref[...] = u[:, :d] * _sigmoid(u[:, d:])


def _pre_conv(xs, mod, nw, w, b, *, nt, mrow):
    d = xs.shape[1]
    tm = TOKEN_TILE
    full = lambda a: pl.BlockSpec(a.shape, lambda i: (0,) * a.ndim)
    return pl.pallas_call(
        _pre_conv_kernel,
        grid=(nt,),
        in_specs=[pl.BlockSpec((tm, d), lambda i: (i, 0)),
                  pl.BlockSpec((1, 6, d), lambda i: (mrow(i), 0, 0)),
                  full(nw), full(w), full(b)],
        out_specs=pl.BlockSpec((tm, d), lambda i: (i, 0)),
        out_shape=jax.ShapeDtypeStruct((nt * tm, d), F32),
        compiler_params=_cparams(("arbitrary",)),
        name="pre_conv",
    )(xs, mod, nw, w, b)


def _conv_kernel(prev_ref, cur_ref, next_ref, dw_ref, db_ref, lw_ref, lb_ref, o_ref, win, conv,
                 *, tpb):
    tm, d = cur_ref.shape
    r = pl.program_id(0) % tpb
    halo = CONV_HALO
    pad = CONV_WIDTH // 2
    win[0:halo, :] = jnp.where(r > 0, prev_ref[...], 0.0)
    win[halo:halo + tm, :] = cur_ref[...]
    win[halo + tm:halo + tm + halo, :] = jnp.where(r < tpb - 1, next_ref[...], 0.0)

    rows = 64
    for c in range(d // LANES):
        cs = slice(c * LANES, (c + 1) * LANES)
        taps = dw_ref[:, cs]
        bias = db_ref[:, cs]

        def chunk(q, carry):
            r0 = pl.multiple_of(q * rows, rows)
            slab = win[pl.ds(r0, rows + 2 * halo), cs]
            acc = jnp.zeros((rows, LANES), F32) + bias
            for b in range(SUBLANES):
                part = None
                for j in range(CONV_WIDTH):
                    off = halo - pad + j
                    if off % SUBLANES == b:
                        a = off - b
                        term = taps[j:j + 1, :] * slab[a:a + rows + SUBLANES, :]
                        part = term if part is None else part + term
                acc = acc + part[b:b + rows, :]
            conv[pl.ds(r0, rows), cs] = acc
            return carry

        lax.fori_loop(0, tm // rows, chunk, 0)

    u = conv[...]
    mu = jnp.mean(u, axis=-1, keepdims=True)
    var = jnp.mean(jnp.square(u - mu), axis=-1, keepdims=True)
    y = (u - mu) * lax.rsqrt(var + NORM_EPS) * lw_ref[...] + lb_ref[...]
    o_ref[...] = _silu(y).astype(BF16)


def _conv(glu, dw, db, lw, lb, *, nt, tpb):
    d = glu.shape[1]
    tm = TOKEN_TILE
    hb = tm // CONV_HALO
    n_halo_blocks = glu.shape[0] // CONV_HALO
    full = lambda a: pl.BlockSpec(a.shape, lambda i: (0,) * a.ndim)
    return pl.pallas_call(
        functools.partial(_conv_kernel, tpb=tpb),
        grid=(nt,),
        in_specs=[pl.BlockSpec((CONV_HALO, d), lambda i: (jnp.maximum(i * hb - 1, 0), 0)),
                  pl.BlockSpec((tm, d), lambda i: (i, 0)),
                  pl.BlockSpec((CONV_HALO, d),
                               lambda i: (jnp.minimum((i + 1) * hb, n_halo_blocks - 1), 0)),
                  full(dw), full(db), full(lw), full(lb)],
        out_specs=pl.BlockSpec((tm, d), lambda i: (i, 0)),
        out_shape=jax.ShapeDtypeStruct((nt * tm, d), BF16),
        scratch_shapes=[pltpu.VMEM((tm + 2 * CONV_HALO, d), F32), pltpu.VMEM((tm, d), F32)],
        compiler_params=_cparams(("arbitrary",)),
        name="dw_conv",
    )(glu, glu, glu, dw, db, lw, lb)


def _rope_tables(n_tokens, extra_rows):
    rows = n_tokens // GRID_W
    row = jnp.repeat(jnp.arange(rows, dtype=F32), GRID_W)
    col = jnp.tile(jnp.arange(GRID_W, dtype=F32), rows)
    axis_dim = QK_ROPE_DIM // 2
    inv_freq = ROPE_THETA ** (-jnp.arange(0, axis_dim, 2, dtype=F32) / axis_dim)
    ang_r = row[:, None] * inv_freq
    ang_c = col[:, None] * inv_freq
    ang = jnp.concatenate([ang_r, ang_r, ang_c, ang_c], axis=-1)
    cos = jnp.concatenate([jnp.cos(ang), jnp.ones((extra_rows, QK_ROPE_DIM), F32)], axis=0)
    sin = jnp.concatenate([jnp.sin(ang), jnp.zeros((extra_rows, QK_ROPE_DIM), F32)], axis=0)
    return jnp.tile(cos, (1, N_HEADS)), jnp.tile(sin, (1, N_HEADS))


def kernel(x, c, ctx, c_ctx, mod_w, mod_b, norm_mix_w, norm_ffn_w, mla_wq_a, mla_q_norm, mla_wq_b, mla_wkv_a, mla_kv_norm, mla_wkv_b, mla_wo, conv_pw1_w, conv_pw1_b, conv_dw_w, conv_dw_b, conv_norm_w, conv_norm_b, conv_pw2_w, conv_pw2_b, router_w, router_bias, exp_w1, exp_w3, exp_w2, shared_w1, shared_w3, shared_w2, final_norm_w):
    nb, seq, d = x.shape
    n_ctx = ctx.shape[1]
    tm = TOKEN_TILE
    tpb, cpb = seq // tm, n_ctx // tm
    nx, nc = nb * tpb, nb * cpb
    row = lambda a: a.reshape(1, -1)

    pad_rows = (-(nb + 1)) % SUBLANES
    cvec = jnp.concatenate([c, c_ctx[None, :], jnp.zeros((pad_rows, d), F32)], axis=0)
    mod = _modulation(cvec, mod_w, mod_b).reshape(mod_w.shape[0], cvec.shape[0], 6, d)

    def mrow_all(i):
        return jnp.where(i < nx, i // tpb, nb)

    tok = jnp.concatenate([x.reshape(nb * seq, d), ctx.reshape(nb * n_ctx, d)], axis=0)
    wqb = mla_wq_b[0].reshape(Q_LORA_RANK, N_HEADS, QK_NOPE_DIM + QK_ROPE_DIM)
    wqb = jnp.concatenate([wqb[:, :, :QK_NOPE_DIM].reshape(Q_LORA_RANK, -1),
                           wqb[:, :, QK_NOPE_DIM:].reshape(Q_LORA_RANK, -1)], axis=1)
    wkva = mla_wkv_a[0]
    wkva = jnp.concatenate([wkva[:, :KV_LORA_RANK]]
                           + [wkva[:, KV_LORA_RANK:]] * (LANES // QK_ROPE_DIM), axis=1)
    wkvb = mla_wkv_b[0].reshape(KV_LORA_RANK, N_HEADS, QK_NOPE_DIM + V_HEAD_DIM)
    wkvb = jnp.concatenate([wkvb[:, :, :QK_NOPE_DIM].reshape(KV_LORA_RANK, -1),
                            wkvb[:, :, QK_NOPE_DIM:].reshape(KV_LORA_RANK, -1)], axis=1)
    cos, sin = _rope_tables(seq, tm)
    qn, qp, kcat, v = _pre_mla(
        tok, mod[0], row(norm_mix_w[0]), mla_wq_a[0].astype(BF16), row(mla_q_norm[0]),
        wqb.astype(BF16), wkva.astype(BF16), row(mla_kv_norm[0]), wkvb.astype(BF16), cos, sin,
        nb=nb, tpb=tpb, cpb=cpb)
    o = _attention(qn, qp, kcat, v, nb=nb, tpb=tpb, cpb=cpb)

    moe_w = lambda i: (exp_w1[i], exp_w3[i], exp_w2[i])
    post_w = lambda i: (row(norm_ffn_w[i]), router_w[i].T, router_bias[i].reshape(-1, 1),
                        shared_w1[i].astype(BF16), shared_w3[i].astype(BF16),
                        shared_w2[i].astype(BF16))
    ptile = POST_TILE if seq % POST_TILE == 0 and (nb * n_ctx) % POST_TILE == 0 else tm
    ppb = seq // ptile
    h2, base, idx8, wt8, rank8, cnt = _post(
        o, mla_wo[0].astype(BF16), jnp.zeros((1, d), F32), tok, mod[0], *post_w(0),
        tm=ptile, nt=(nb * (seq + n_ctx)) // ptile,
        mrow=lambda i: jnp.where(i < nb * ppb, i // ppb, nb), pair_major=True)
    xs = _moe(h2, base, idx8, wt8, rank8, cnt, mod[0], row(final_norm_w), *moe_w(0),
              nt_out=nx + nc, mrow=mrow_all, final_norm=False)

    mrow_x = lambda i: i // tpb
    glu = _pre_conv(xs, mod[1], row(norm_mix_w[1]), conv_pw1_w[0].astype(BF16),
                    row(conv_pw1_b[0]), nt=nx, mrow=mrow_x)
    taps = jnp.concatenate([conv_dw_w[0], jnp.zeros((1, d), F32)], axis=0)
    act = _conv(glu, taps, row(conv_dw_b[0]), row(conv_norm_w[0]), row(conv_norm_b[0]),
                nt=nx, tpb=tpb)
    h2, base, idx8, wt8, rank8, cnt = _post(
        act, conv_pw2_w[0].astype(BF16), row(conv_pw2_b[0]), xs, mod[1],
        *post_w(1), tm=ptile, nt=nb * ppb, mrow=lambda i: i // ppb, pair_major=False)
    out = _moe(h2, base, idx8, wt8, rank8, cnt, mod[1], row(final_norm_w), *moe_w(1),
               nt_out=nx, mrow=mrow_x, final_norm=True)
    return out.reshape(nb, seq, d)
```

```python
import functools

import jax
import jax.numpy as jnp
from jax import lax
from jax.experimental import pallas as pl
from jax.experimental.pallas import tpu as pltpu

N_HEADS = 16
QK_NOPE_DIM = 64
QK_ROPE_DIM = 32
V_HEAD_DIM = 64
Q_LORA_RANK = 512
KV_LORA_RANK = 256
ROPE_THETA = 10000.0
GRID_W = 64
CONV_WIDTH = 31
N_EXPERTS = 64
TOP_K = 6
N_GROUPS = 8
TOPK_GROUPS = 4
ROUTED_SCALE = 2.5
NORM_EPS = 1e-6
LOG2_E = 1.4426950408889634

LANES = 128
SUBLANES = 8
TOKEN_TILE = 256
POST_TILE = 512
EXPERT_ROWS = 512
DISPATCH_TILE = 256
COMBINE_TILE = 256
ISSUE_UNROLL = 4
CONV_HALO = 16
N_PAIRS = N_HEADS // 2
N_ROPE_GROUPS = N_HEADS // 4
VMEM_LIMIT = 56 * 1024 * 1024

F32 = jnp.float32
BF16 = jnp.bfloat16


def _cparams(sem):
    return pltpu.CompilerParams(dimension_semantics=sem, vmem_limit_bytes=VMEM_LIMIT)


def _rms(x, w):
    return x * lax.rsqrt(jnp.mean(x * x, axis=-1, keepdims=True) + NORM_EPS) * w


def _sigmoid(x):
    return 1.0 / (1.0 + jnp.exp(-x))


def _silu(x):
    return x * _sigmoid(x)


def _dot(a, b):
    return jnp.dot(a, b, preferred_element_type=F32)


def _pack_bf16_pairs(x):
    n = x.shape[1] // 2
    hi = lax.bitcast_convert_type(x[:, :n].astype(BF16).astype(F32), jnp.uint32)
    lo = lax.bitcast_convert_type(x[:, n:].astype(BF16).astype(F32), jnp.uint32)
    return hi | (lo >> 16)


def _unpack_bf16_pairs(p):
    hi = lax.bitcast_convert_type(p & jnp.uint32(0xFFFF0000), F32)
    lo = lax.bitcast_convert_type(p << 16, F32)
    return jnp.concatenate([hi, lo], axis=1)


def _store_row_groups(ref, packed):
    m, n = packed.shape
    g = n // LANES
    for c in range(g):
        ref[pl.ds(c, m, stride=g), :] = packed[:, c * LANES:(c + 1) * LANES]


def _load_row_groups(ref, m, g, start=0):
    return jnp.concatenate([ref[pl.ds(start + c, m, stride=g), :] for c in range(g)], axis=1)


def _mod_kernel(c_ref, w_ref, b_ref, o_ref):
    c = c_ref[...]
    o_ref[0] = jnp.dot(_silu(c), w_ref[0], preferred_element_type=F32,
                       precision=lax.Precision.HIGHEST) + b_ref[0]


def _modulation(cvec, mod_w, mod_b):
    depth, d, n = mod_w.shape
    rows = cvec.shape[0]
    tn = 1536
    return pl.pallas_call(
        _mod_kernel,
        grid=(depth, n // tn),
        in_specs=[pl.BlockSpec((rows, d), lambda l, j: (0, 0)),
                  pl.BlockSpec((1, d, tn), lambda l, j: (l, 0, j)),
                  pl.BlockSpec((1, 1, tn), lambda l, j: (l, 0, j))],
        out_specs=pl.BlockSpec((1, rows, tn), lambda l, j: (l, 0, j)),
        out_shape=jax.ShapeDtypeStruct((depth, rows, n), F32),
        compiler_params=_cparams(("arbitrary", "arbitrary")),
        name="modulation",
    )(cvec, mod_w, mod_b.reshape(depth, 1, n))


def _rope(v, cos, sin):
    n = v.shape[1]
    lane = lax.broadcasted_iota(jnp.int32, v.shape, 1)
    up = pltpu.roll(v, 8, axis=1)
    dn = pltpu.roll(v, n - 8, axis=1)
    rot = jnp.where(lane % 16 < 8, -dn, up)
    return v * cos + rot * sin


def _pre_mla_kernel(tok_ref, mod_ref, nw_ref, wqa_ref, qn_ref, wqb_ref, wkva_ref, kvn_ref,
                    wkvb_ref, cos_ref, sin_ref, qn_out, qp_out, k_out, v_out):
    x = tok_ref[...]
    m = mod_ref[0]
    h = _rms(x, nw_ref[...]) * (1.0 + m[1:2]) + m[0:1]
    hb = h.astype(BF16)
    cos = cos_ref[...]
    sin = sin_ref[...]

    qa = _rms(_dot(hb, wqa_ref[...]), qn_ref[...])
    scale = (QK_NOPE_DIM + QK_ROPE_DIM) ** -0.5 * LOG2_E
    q = _dot(qa.astype(BF16), wqb_ref[...]) * scale
    d_nope = N_HEADS * QK_NOPE_DIM
    q_pe = _rope(q[:, d_nope:], cos, sin)

    kva = _dot(hb, wkva_ref[...])
    kpe4_t = _rope(kva[:, KV_LORA_RANK:], cos[:, :LANES], sin[:, :LANES]).T.astype(BF16)
    ckv = _rms(kva[:, :KV_LORA_RANK], kvn_ref[...])
    kv = _dot(ckv.astype(BF16), wkvb_ref[...])

    for j in range(N_PAIRS):
        sl = slice(j * LANES, (j + 1) * LANES)
        qn_out[j] = q[:, sl].astype(BF16)
        k_out[j, :LANES, :] = kv[:, sl].T.astype(BF16)
        k_out[j, LANES:, :] = kpe4_t
        v_out[j] = kv[:, d_nope + j * LANES:d_nope + (j + 1) * LANES].astype(BF16)
    for g in range(N_ROPE_GROUPS):
        qp_out[g] = q_pe[:, g * LANES:(g + 1) * LANES].astype(BF16)


def _pre_mla(tok, mod, nw, wqa, qn, wqb, wkva, kvn, wkvb, cos, sin, *, nb, tpb, cpb):
    t, d = tok.shape
    tm = TOKEN_TILE
    nx = nb * tpb
    spb = tpb + cpb

    def mrow(i):
        return jnp.where(i < nx, i // tpb, nb)

    def kvblk(i):
        ic = i - nx
        return jnp.where(i < nx, (i // tpb) * spb + cpb + i % tpb, (ic // cpb) * spb + ic % cpb)

    def ropeblk(i):
        return jnp.where(i < nx, i % tpb, tpb)

    full = lambda a: pl.BlockSpec(a.shape, lambda i: (0,) * a.ndim)
    s_rows = nb * spb * tm
    return pl.pallas_call(
        _pre_mla_kernel,
        grid=(t // tm,),
        in_specs=[pl.BlockSpec((tm, d), lambda i: (i, 0)),
                  pl.BlockSpec((1, 6, d), lambda i: (mrow(i), 0, 0)),
                  full(nw), full(wqa), full(qn), full(wqb), full(wkva), full(kvn), full(wkvb),
                  pl.BlockSpec((tm, cos.shape[1]), lambda i: (ropeblk(i), 0)),
                  pl.BlockSpec((tm, sin.shape[1]), lambda i: (ropeblk(i), 0))],
        out_specs=[pl.BlockSpec((N_PAIRS, tm, LANES), lambda i: (0, i, 0)),
                   pl.BlockSpec((N_ROPE_GROUPS, tm, LANES), lambda i: (0, i, 0)),
                   pl.BlockSpec((N_PAIRS, 2 * LANES, tm), lambda i: (0, 0, kvblk(i))),
                   pl.BlockSpec((N_PAIRS, tm, LANES), lambda i: (0, kvblk(i), 0))],
        out_shape=[jax.ShapeDtypeStruct((N_PAIRS, t, LANES), BF16),
                   jax.ShapeDtypeStruct((N_ROPE_GROUPS, t, LANES), BF16),
                   jax.ShapeDtypeStruct((N_PAIRS, 2 * LANES, s_rows), BF16),
                   jax.ShapeDtypeStruct((N_PAIRS, s_rows, LANES), BF16)],
        compiler_params=_cparams(("arbitrary",)),
        name="pre_mla",
    )(tok, mod, nw, wqa, qn, wqb, wkva, kvn, wkvb, cos, sin)


def _attn_tile(qn_ref, qp_ref, k_ref, v_ref, o_ref, s_len):
    tq = qn_ref.shape[1]
    lane = lax.broadcasted_iota(jnp.int32, (tq, LANES), 1)

    def pair(j, carry):
        qn = qn_ref[j]
        qp = qp_ref[j // 2]
        k = k_ref[j, :, :s_len]
        v = v_ref[j, :s_len, :]
        outs = []
        for e in range(2):
            qn_m = jnp.where(lane // QK_NOPE_DIM == e, qn, jnp.zeros_like(qn))
            qp_m = jnp.where(lane // QK_ROPE_DIM == 2 * (j % 2) + e, qp, jnp.zeros_like(qp))
            lhs = jnp.concatenate([qn_m, qp_m], axis=1)
            s = _dot(lhs, k)
            p = jnp.exp2(s - jnp.max(s, axis=-1, keepdims=True))
            l = jnp.sum(p, axis=-1, keepdims=True)
            outs.append(_dot(p.astype(BF16), v) * (1.0 / l))
        o_ref[j] = jnp.where(lane < V_HEAD_DIM, outs[0], outs[1]).astype(BF16)
        return carry

    lax.fori_loop(0, N_PAIRS, pair, 0, unroll=4)


def _attn_kernel(qn_ref, qp_ref, k_ref, v_ref, o_ref, *, tpb, n_ctx):
    step = pl.program_id(1)

    @pl.when(step < tpb)
    def _():
        _attn_tile(qn_ref, qp_ref, k_ref, v_ref, o_ref, v_ref.shape[1])

    @pl.when(step >= tpb)
    def _():
        _attn_tile(qn_ref, qp_ref, k_ref, v_ref, o_ref, n_ctx)


def _attention(qn, qp, kcat, v, *, nb, tpb, cpb):
    t = qn.shape[1]
    tq = TOKEN_TILE
    nx = nb * tpb
    s_len = (tpb + cpb) * tq

    def qblk(b, i):
        return jnp.where(i < tpb, b * tpb + i, nx + b * cpb + (i - tpb))

    return pl.pallas_call(
        functools.partial(_attn_kernel, tpb=tpb, n_ctx=cpb * tq),
        grid=(nb, tpb + cpb),
        in_specs=[pl.BlockSpec((N_PAIRS, tq, LANES), lambda b, i: (0, qblk(b, i), 0)),
                  pl.BlockSpec((N_ROPE_GROUPS, tq, LANES), lambda b, i: (0, qblk(b, i), 0)),
                  pl.BlockSpec((N_PAIRS, 2 * LANES, s_len), lambda b, i: (0, 0, b)),
                  pl.BlockSpec((N_PAIRS, s_len, LANES), lambda b, i: (0, b, 0))],
        out_specs=pl.BlockSpec((N_PAIRS, tq, LANES), lambda b, i: (0, qblk(b, i), 0)),
        out_shape=jax.ShapeDtypeStruct((N_PAIRS, t, LANES), BF16),
        compiler_params=_cparams(("arbitrary", "arbitrary")),
        name="attention",
    )(qn, qp, kcat, v)


def _post_kernel(act_ref, wmix_ref, bmix_ref, xs_ref, mod_ref, nfw_ref, rwt_ref, rb_ref,
                 sw1_ref, sw3_ref, sw2_ref,
                 h2_out, base_out, idx_out, wt_out, rank_out, cnt_out, cnt_scr, *, pair_major):
    i = pl.program_id(0)
    tm = xs_ref.shape[0]

    @pl.when(i == 0)
    def _():
        cnt_scr[...] = jnp.zeros_like(cnt_scr)

    if pair_major:
        act = jnp.concatenate([act_ref[j] for j in range(N_PAIRS)], axis=1)
    else:
        act = act_ref[...]
    m = mod_ref[0]
    mix = _dot(act, wmix_ref[...]) + bmix_ref[...]
    xs = xs_ref[...] + m[2:3] * mix
    h2 = _rms(xs, nfw_ref[...]) * (1.0 + m[4:5]) + m[3:4]
    _store_row_groups(h2_out, _pack_bf16_pairs(h2))

    h2b = h2.astype(BF16)
    hid = _silu(_dot(h2b, sw1_ref[...])) * _dot(h2b, sw3_ref[...])
    base_out[...] = xs + m[5:6] * _dot(hid.astype(BF16), sw2_ref[...])

    logits = lax.dot_general(rwt_ref[...], h2, (((1,), (1,)), ((), ())),
                             preferred_element_type=F32, precision=lax.Precision.HIGHEST)
    scores = _sigmoid(logits)
    sel = scores + rb_ref[...]
    per_group = N_EXPERTS // N_GROUPS
    neg = jnp.float32(-jnp.inf)
    sub_iota = lax.broadcasted_iota(jnp.int32, (per_group, tm), 0)
    gs_rows = []
    for g in range(N_GROUPS):
        sg = sel[g * per_group:(g + 1) * per_group, :]
        m1 = jnp.max(sg, axis=0, keepdims=True)
        first = jnp.min(jnp.where(sg == m1, sub_iota, per_group), axis=0, keepdims=True)
        m2 = jnp.max(jnp.where(sub_iota == first, neg, sg), axis=0, keepdims=True)
        gs_rows.append(m1 + m2)
    gs = jnp.concatenate(gs_rows, axis=0)
    g_iota = lax.broadcasted_iota(jnp.int32, (N_GROUPS, tm), 0)
    gmask = jnp.zeros((N_GROUPS, tm), F32)
    work = gs
    for _ in range(TOPK_GROUPS):
        mx = jnp.max(work, axis=0, keepdims=True)
        gi = jnp.min(jnp.where(work == mx, g_iota, N_GROUPS), axis=0, keepdims=True)
        pick = g_iota == gi
        gmask = jnp.where(pick, 1.0, gmask)
        work = jnp.where(pick, neg, work)
    emask = jnp.concatenate(
        [jnp.broadcast_to(gmask[g:g + 1, :], (per_group, tm)) for g in range(N_GROUPS)], axis=0)
    masked = jnp.where(emask > 0.5, sel, neg)
    e_iota = lax.broadcasted_iota(jnp.int32, (N_EXPERTS, tm), 0)
    picks, idx_rows, w_rows = [], [], []
    for _ in range(TOP_K):
        mx = jnp.max(masked, axis=0, keepdims=True)
        ei = jnp.min(jnp.where(masked == mx, e_iota, N_EXPERTS), axis=0, keepdims=True)
        pick = e_iota == ei
        picks.append(pick)
        idx_rows.append(ei)
        w_rows.append(jnp.sum(jnp.where(pick, scores, 0.0), axis=0, keepdims=True))
        masked = jnp.where(pick, neg, masked)
    wsum = w_rows[0]
    for r in w_rows[1:]:
        wsum = wsum + r
    wnorm = ROUTED_SCALE / wsum

    member = jnp.zeros((N_EXPERTS, tm), F32)
    for pick in picks:
        member = jnp.where(pick, 1.0, member)
    before = (lax.broadcasted_iota(jnp.int32, (tm, tm), 0)
              < lax.broadcasted_iota(jnp.int32, (tm, tm), 1))
    prefix = _dot(member.astype(BF16), jnp.where(before, 1.0, 0.0).astype(BF16))
    cnt = cnt_scr[...]
    rank_full = cnt[:, 0:1] + prefix
    rank_rows = [jnp.sum(jnp.where(pick, rank_full, 0.0), axis=0, keepdims=True)
                 for pick in picks]
    cnt = cnt + jnp.sum(member, axis=1, keepdims=True)
    cnt_scr[...] = cnt
    cnt_out[...] = cnt

    pad = SUBLANES - TOP_K
    zi = jnp.zeros((pad, tm), jnp.int32)
    zf = jnp.zeros((pad, tm), F32)
    idx_out[...] = jnp.concatenate(idx_rows + [zi], axis=0)
    wt_out[...] = jnp.concatenate([r * wnorm for r in w_rows] + [zf], axis=0)
    rank_out[...] = jnp.concatenate([r.astype(jnp.int32) for r in rank_rows] + [zi], axis=0)


def _post(act, wmix, bmix, xs, mod, nfw, rwt, rb, sw1, sw3, sw2, *, tm, nt, mrow, pair_major):
    d = xs.shape[1]
    t = nt * tm
    g = d // 2 // LANES
    full = lambda a: pl.BlockSpec(a.shape, lambda i: (0,) * a.ndim)
    if pair_major:
        act_spec = pl.BlockSpec((N_PAIRS, tm, LANES), lambda i: (0, i, 0))
    else:
        act_spec = pl.BlockSpec((tm, act.shape[1]), lambda i: (i, 0))
    row8 = pl.BlockSpec((SUBLANES, tm), lambda i: (0, i))
    return pl.pallas_call(
        functools.partial(_post_kernel, pair_major=pair_major),
        grid=(nt,),
        in_specs=[act_spec, full(wmix), full(bmix),
                  pl.BlockSpec((tm, d), lambda i: (i, 0)),
                  pl.BlockSpec((1, 6, d), lambda i: (mrow(i), 0, 0)),
                  full(nfw), full(rwt), full(rb), full(sw1), full(sw3), full(sw2)],
        out_specs=[pl.BlockSpec((tm * g, LANES), lambda i: (i, 0)),
                   pl.BlockSpec((tm, d), lambda i: (i, 0)),
                   row8, row8, row8,
                   pl.BlockSpec((N_EXPERTS, LANES), lambda i: (0, 0))],
        out_shape=[jax.ShapeDtypeStruct((t * g, LANES), jnp.uint32),
                   jax.ShapeDtypeStruct((t, d), F32),
                   jax.ShapeDtypeStruct((SUBLANES, t), jnp.int32),
                   jax.ShapeDtypeStruct((SUBLANES, t), F32),
                   jax.ShapeDtypeStruct((SUBLANES, t), jnp.int32),
                   jax.ShapeDtypeStruct((N_EXPERTS, LANES), F32)],
        scratch_shapes=[pltpu.VMEM((N_EXPERTS, LANES), F32)],
        compiler_params=_cparams(("arbitrary",)),
        name="post_mixer",
    )(act, wmix, bmix, xs, mod, nfw, rwt, rb, sw1, sw3, sw2)


def _dispatch_kernel(dest_ref, fill_ref, h2_ref, xg_out, zbuf, sem, zsem, *, t_total, tile, g):
    step = pl.program_id(0)
    base = step * tile
    rows = zbuf.shape[0]
    n_blocks = xg_out.shape[0] // rows

    @pl.when(step == 0)
    def _():
        zbuf[...] = jnp.zeros_like(zbuf)

        def fill_copy(n):
            return pltpu.make_async_copy(
                zbuf, xg_out.at[pl.ds(pl.multiple_of(n * rows, rows), rows)], zsem)

        def start(n, carry):
            @pl.when(fill_ref[n] > 0)
            def _():
                fill_copy(n).start()
            return carry

        def wait(n, carry):
            @pl.when(fill_ref[n] > 0)
            def _():
                fill_copy(n).wait()
            return carry

        lax.fori_loop(0, n_blocks, start, 0)
        lax.fori_loop(0, n_blocks, wait, 0)

    def row_copy(r, slot):
        return pltpu.make_async_copy(h2_ref.at[pl.ds(pl.multiple_of(r * g, g), g)],
                                     xg_out.at[pl.ds(pl.multiple_of(slot * g, g), g)], sem)

    def issue(r, carry):
        for k in range(TOP_K):
            row_copy(r, dest_ref[k * t_total + base + r]).start(priority=k % 2)
        return carry

    lax.fori_loop(0, tile, issue, 0, unroll=ISSUE_UNROLL)

    def drain(r, carry):
        for k in range(TOP_K):
            row_copy(0, 0).wait()
        return carry

    lax.fori_loop(0, tile, drain, 0, unroll=ISSUE_UNROLL)


def _dispatch(dest, fill, h2, n_slots, g):
    t = h2.shape[0] // g
    tile = DISPATCH_TILE
    return pl.pallas_call(
        functools.partial(_dispatch_kernel, t_total=t, tile=tile, g=g),
        grid_spec=pltpu.PrefetchScalarGridSpec(
            num_scalar_prefetch=2,
            grid=(t // tile,),
            in_specs=[pl.BlockSpec((tile * g, LANES), lambda i, dr, fr: (i, 0))],
            out_specs=pl.BlockSpec(memory_space=pl.ANY),
            scratch_shapes=[pltpu.VMEM((EXPERT_ROWS * g, LANES), h2.dtype),
                            pltpu.SemaphoreType.DMA(()), pltpu.SemaphoreType.DMA(())]),
        out_shape=jax.ShapeDtypeStruct((n_slots * g, LANES), h2.dtype),
        compiler_params=_cparams(("arbitrary",)),
        name="moe_dispatch",
    )(dest, fill, h2)


def _expert_kernel(be_ref, nu_ref, x_ref, w1_ref, w3_ref, w2_ref, y_ref, w1b, w3b, w2b, *, g):
    n = pl.program_id(0)
    used = n < nu_ref[0]
    rows = x_ref.shape[0] // g

    @pl.when(jnp.logical_or(n == 0, be_ref[n] != be_ref[jnp.maximum(n - 1, 0)]))
    def _():
        w1b[...] = w1_ref[0, 0].astype(BF16)
        w3b[...] = w3_ref[0, 0].astype(BF16)
        w2b[...] = w2_ref[0, 0].astype(BF16)

    @pl.when(used)
    def _():
        x = _unpack_bf16_pairs(_load_row_groups(x_ref, rows, g)).astype(BF16)
        hid = _silu(_dot(x, w1b[...])) * _dot(x, w3b[...])
        _store_row_groups(y_ref, _pack_bf16_pairs(_dot(hid.astype(BF16), w2b[...])))

    @pl.when(jnp.logical_not(used))
    def _():
        y_ref[...] = jnp.zeros_like(y_ref)


def _experts(blk_e, n_used, xg, w1, w3, w2, g, layer):
    rows = EXPERT_ROWS * g
    nblk = xg.shape[0] // rows
    _, _, d, e_dim = w1.shape

    def xmap(n, be, nu):
        return (jnp.minimum(n, nu[0] - 1), 0)

    def wmap(n, be, nu):
        return (layer, be[n], 0, 0)

    return pl.pallas_call(
        functools.partial(_expert_kernel, g=g),
        grid_spec=pltpu.PrefetchScalarGridSpec(
            num_scalar_prefetch=2,
            grid=(nblk,),
            in_specs=[pl.BlockSpec((rows, LANES), xmap),
                      pl.BlockSpec((1, 1, d, e_dim), wmap),
                      pl.BlockSpec((1, 1, d, e_dim), wmap),
                      pl.BlockSpec((1, 1, e_dim, d), wmap)],
            out_specs=pl.BlockSpec((rows, LANES), lambda n, be, nu: (n, 0)),
            scratch_shapes=[pltpu.VMEM((d, e_dim), BF16), pltpu.VMEM((d, e_dim), BF16),
                            pltpu.VMEM((e_dim, d), BF16)]),
        out_shape=jax.ShapeDtypeStruct(xg.shape, jnp.uint32),
        compiler_params=_cparams(("arbitrary",)),
        name="moe_experts",
    )(blk_e, n_used, xg, w1, w3, w2)


def _combine_kernel(dest_ref, y_hbm, base_ref, wt_ref, mod_ref, fnw_ref, o_ref, buf, sems,
                    *, t_total, n_tiles, final_norm, g):
    tm = base_ref.shape[0]
    step = pl.program_id(0)
    cur = step % 2
    half = TOP_K * tm * g
    chunk = SUBLANES
    gate = mod_ref[0][5:6]
    fnw = fnw_ref[...]

    def row_copy(slot, which, k, r):
        dst = pl.multiple_of(which * half + (k * tm + r) * g, g)
        return pltpu.make_async_copy(y_hbm.at[pl.ds(pl.multiple_of(slot * g, g), g)],
                                     buf.at[pl.ds(dst, g)], sems.at[which])

    def issue_chunk(tile, which, r0):
        for dr in range(chunk):
            for k in range(TOP_K):
                slot = dest_ref[k * t_total + tile * tm + r0 + dr]
                row_copy(slot, which, k, r0 + dr).start(priority=k % 2)

    def sum_chunk(r0):
        wt = wt_ref[pl.ds(r0, chunk), :]
        acc = None
        for k in range(TOP_K):
            start = cur * half + (k * tm + r0) * g
            packed = _load_row_groups(buf, chunk, g, start=start)
            term = wt[:, k:k + 1] * _unpack_bf16_pairs(packed)
            acc = term if acc is None else acc + term
        out = base_ref[pl.ds(r0, chunk), :] + gate * acc
        if final_norm:
            out = _rms(out, fnw)
        o_ref[pl.ds(r0, chunk), :] = out

    @pl.when(step == 0)
    def _():
        def first(c, carry):
            issue_chunk(0, 0, pl.multiple_of(c * chunk, chunk))
            return carry
        lax.fori_loop(0, tm // chunk, first, 0)

    def drain(r, carry):
        for k in range(TOP_K):
            row_copy(0, cur, k, 0).wait()
        return carry

    lax.fori_loop(0, tm, drain, 0, unroll=ISSUE_UNROLL)

    @pl.when(step + 1 < n_tiles)
    def _():
        def body(c, carry):
            r0 = pl.multiple_of(c * chunk, chunk)
            sum_chunk(r0)
            issue_chunk(step + 1, 1 - cur, r0)
            return carry
        lax.fori_loop(0, tm // chunk, body, 0)

    @pl.when(step + 1 == n_tiles)
    def _():
        def body(c, carry):
            sum_chunk(pl.multiple_of(c * chunk, chunk))
            return carry
        lax.fori_loop(0, tm // chunk, body, 0)


def _combine(dest, yg, base, wt_t, mod, fnw, *, nt, mrow, final_norm, g):
    d = base.shape[1]
    tm = COMBINE_TILE
    t = nt * tm
    return pl.pallas_call(
        functools.partial(_combine_kernel, t_total=base.shape[0], n_tiles=nt,
                          final_norm=final_norm, g=g),
        grid_spec=pltpu.PrefetchScalarGridSpec(
            num_scalar_prefetch=1,
            grid=(nt,),
            in_specs=[pl.BlockSpec(memory_space=pl.ANY),
                      pl.BlockSpec((tm, d), lambda i, dr: (i, 0)),
                      pl.BlockSpec((tm, SUBLANES), lambda i, dr: (i, 0)),
                      pl.BlockSpec((1, 6, d), lambda i, dr: (mrow(i), 0, 0)),
                      pl.BlockSpec((1, d), lambda i, dr: (0, 0))],
            out_specs=pl.BlockSpec((tm, d), lambda i, dr: (i, 0)),
            scratch_shapes=[pltpu.VMEM((2 * TOP_K * tm * g, LANES), jnp.uint32),
                            pltpu.SemaphoreType.DMA((2,))]),
        out_shape=jax.ShapeDtypeStruct((t, d), F32),
        compiler_params=_cparams(("arbitrary",)),
        name="moe_combine",
    )(dest, yg, base, wt_t, mod, fnw)


def _moe(h2, base, idx8, wt8, rank8, cnt, mod, fnw, w1, w3, w2, *, layer, nt_out, mrow,
         final_norm):
    g = base.shape[1] // 2 // LANES
    t = h2.shape[0] // g
    rows = EXPERT_ROWS
    counts = cnt[:, 0].astype(jnp.int32)
    padded = (counts + rows - 1) // rows * rows
    pad_end = jnp.cumsum(padded)
    pad_start = pad_end - padded
    nblk = (t * TOP_K + rows - 1) // rows + N_EXPERTS
    e_ids = jnp.arange(N_EXPERTS, dtype=jnp.int32)
    idx = idx8[:TOP_K]
    start_of = jnp.sum(jnp.where(idx[:, :, None] == e_ids, pad_start, 0), axis=-1)
    dest = (start_of + rank8[:TOP_K]).reshape(-1)
    n_used = (pad_end[-1] // rows).astype(jnp.int32)
    blk_start = jnp.arange(nblk, dtype=jnp.int32) * rows
    e_of_blk = jnp.sum((pad_end[None, :] <= blk_start[:, None]).astype(jnp.int32), axis=1)
    valid_end = jnp.sum(jnp.where(e_of_blk[:, None] == e_ids, pad_start + counts, 0), axis=1)
    fill = (blk_start + rows > valid_end).astype(jnp.int32)
    e_last = jnp.max(jnp.where(counts > 0, e_ids, 0))
    blk_e = jnp.minimum(e_of_blk, e_last).astype(jnp.int32)
    xg = _dispatch(dest, fill, h2, nblk * rows, g)
    yg = _experts(blk_e, n_used.reshape(1), xg, w1, w3, w2, g, layer)
    tile_ratio = TOKEN_TILE // COMBINE_TILE
    return _combine(dest, yg, base, wt8.T, mod, fnw, nt=nt_out * tile_ratio,
                    mrow=lambda i: mrow(i // tile_ratio), final_norm=final_norm, g=g)


def _pre_conv_kernel(xs_ref, mod_ref, nw_ref, w_ref, b_ref, o_ref):
    m = mod_ref[0]
    h = _rms(xs_ref[...], nw_ref[...]) * (1.0 + m[1:2]) + m[0:1]
    u = _dot(h.astype(BF16), w_ref[...]) + b_ref[...]
    d = o_ref.shape[1]
    o_ref[...] = u[:, :d] * _sigmoid(u[:, d:])


def _pre_conv(xs, mod, nw, w, b, *, nt, mrow):
    d = xs.shape[1]
    tm = TOKEN_TILE
    full = lambda a: pl.BlockSpec(a.shape, lambda i: (0,) * a.ndim)
    return pl.pallas_call(
        _pre_conv_kernel,
        grid=(nt,),
        in_specs=[pl.BlockSpec((tm, d), lambda i: (i, 0)),
                  pl.BlockSpec((1, 6, d), lambda i: (mrow(i), 0, 0)),
                  full(nw), full(w), full(b)],
        out_specs=pl.BlockSpec((tm, d), lambda i: (i, 0)),
        out_shape=jax.ShapeDtypeStruct((nt * tm, d), F32),
        compiler_params=_cparams(("arbitrary",)),
        name="pre_conv",
    )(xs, mod, nw, w, b)


def _conv_kernel(prev_ref, cur_ref, next_ref, dw_ref, db_ref, lw_ref, lb_ref, o_ref, win, conv,
                 *, tpb):
    tm, d = cur_ref.shape
    r = pl.program_id(0) % tpb
    halo = CONV_HALO
    pad = CONV_WIDTH // 2
    win[0:halo, :] = jnp.where(r > 0, prev_ref[...], 0.0)
    win[halo:halo + tm, :] = cur_ref[...]
    win[halo + tm:halo + tm + halo, :] = jnp.where(r < tpb - 1, next_ref[...], 0.0)

    rows = 64
    for c in range(d // LANES):
        cs = slice(c * LANES, (c + 1) * LANES)
        taps = dw_ref[:, cs]
        bias = db_ref[:, cs]

        def chunk(q, carry):
            r0 = pl.multiple_of(q * rows, rows)
            slab = win[pl.ds(r0, rows + 2 * halo), cs]
            acc = jnp.zeros((rows, LANES), F32) + bias
            for b in range(SUBLANES):
                part = None
                for j in range(CONV_WIDTH):
                    off = halo - pad + j
                    if off % SUBLANES == b:
                        a = off - b
                        term = taps[j:j + 1, :] * slab[a:a + rows + SUBLANES, :]
                        part = term if part is None else part + term
                acc = acc + part[b:b + rows, :]
            conv[pl.ds(r0, rows), cs] = acc
            return carry

        lax.fori_loop(0, tm // rows, chunk, 0)

    u = conv[...]
    mu = jnp.mean(u, axis=-1, keepdims=True)
    var = jnp.mean(jnp.square(u - mu), axis=-1, keepdims=True)
    y = (u - mu) * lax.rsqrt(var + NORM_EPS) * lw_ref[...] + lb_ref[...]
    o_ref[...] = _silu(y).astype(BF16)


def _conv(glu, dw, db, lw, lb, *, nt, tpb):
    d = glu.shape[1]
    tm = TOKEN_TILE
    hb = tm // CONV_HALO
    n_halo_blocks = glu.shape[0] // CONV_HALO
    full = lambda a: pl.BlockSpec(a.shape, lambda i: (0,) * a.ndim)
    return pl.pallas_call(
        functools.partial(_conv_kernel, tpb=tpb),
        grid=(nt,),
        in_specs=[pl.BlockSpec((CONV_HALO, d), lambda i: (jnp.maximum(i * hb - 1, 0), 0)),
                  pl.BlockSpec((tm, d), lambda i: (i, 0)),
                  pl.BlockSpec((CONV_HALO, d),
                               lambda i: (jnp.minimum((i + 1) * hb, n_halo_blocks - 1), 0)),
                  full(dw), full(db), full(lw), full(lb)],
        out_specs=pl.BlockSpec((tm, d), lambda i: (i, 0)),
        out_shape=jax.ShapeDtypeStruct((nt * tm, d), BF16),
        scratch_shapes=[pltpu.VMEM((tm + 2 * CONV_HALO, d), F32), pltpu.VMEM((tm, d), F32)],
        compiler_params=_cparams(("arbitrary",)),
        name="dw_conv",
    )(glu, glu, glu, dw, db, lw, lb)


def _rope_tables(n_tokens, extra_rows):
    rows = n_tokens // GRID_W
    row = jnp.repeat(jnp.arange(rows, dtype=F32), GRID_W)
    col = jnp.tile(jnp.arange(GRID_W, dtype=F32), rows)
    axis_dim = QK_ROPE_DIM // 2
    inv_freq = ROPE_THETA ** (-jnp.arange(0, axis_dim, 2, dtype=F32) / axis_dim)
    ang_r = row[:, None] * inv_freq
    ang_c = col[:, None] * inv_freq
    ang = jnp.concatenate([ang_r, ang_r, ang_c, ang_c], axis=-1)
    cos = jnp.concatenate([jnp.cos(ang), jnp.ones((extra_rows, QK_ROPE_DIM), F32)], axis=0)
    sin = jnp.concatenate([jnp.sin(ang), jnp.zeros((extra_rows, QK_ROPE_DIM), F32)], axis=0)
    return jnp.tile(cos, (1, N_HEADS)), jnp.tile(sin, (1, N_HEADS))


def kernel(x, c, ctx, c_ctx, mod_w, mod_b, norm_mix_w, norm_ffn_w, mla_wq_a, mla_q_norm, mla_wq_b, mla_wkv_a, mla_kv_norm, mla_wkv_b, mla_wo, conv_pw1_w, conv_pw1_b, conv_dw_w, conv_dw_b, conv_norm_w, conv_norm_b, conv_pw2_w, conv_pw2_b, router_w, router_bias, exp_w1, exp_w3, exp_w2, shared_w1, shared_w3, shared_w2, final_norm_w):
    nb, seq, d = x.shape
    n_ctx = ctx.shape[1]
    tm = TOKEN_TILE
    tpb, cpb = seq // tm, n_ctx // tm
    nx, nc = nb * tpb, nb * cpb
    row = lambda a: a.reshape(1, -1)

    pad_rows = (-(nb + 1)) % SUBLANES
    cvec = jnp.concatenate([c, c_ctx[None, :], jnp.zeros((pad_rows, d), F32)], axis=0)
    mod = _modulation(cvec, mod_w, mod_b).reshape(mod_w.shape[0], cvec.shape[0], 6, d)

    def mrow_all(i):
        return jnp.where(i < nx, i // tpb, nb)

    tok = jnp.concatenate([x.reshape(nb * seq, d), ctx.reshape(nb * n_ctx, d)], axis=0)
    wqb = mla_wq_b[0].reshape(Q_LORA_RANK, N_HEADS, QK_NOPE_DIM + QK_ROPE_DIM)
    wqb = jnp.concatenate([wqb[:, :, :QK_NOPE_DIM].reshape(Q_LORA_RANK, -1),
                           wqb[:, :, QK_NOPE_DIM:].reshape(Q_LORA_RANK, -1)], axis=1)
    wkva = mla_wkv_a[0]
    wkva = jnp.concatenate([wkva[:, :KV_LORA_RANK]]
                           + [wkva[:, KV_LORA_RANK:]] * (LANES // QK_ROPE_DIM), axis=1)
    wkvb = mla_wkv_b[0].reshape(KV_LORA_RANK, N_HEADS, QK_NOPE_DIM + V_HEAD_DIM)
    wkvb = jnp.concatenate([wkvb[:, :, :QK_NOPE_DIM].reshape(KV_LORA_RANK, -1),
                            wkvb[:, :, QK_NOPE_DIM:].reshape(KV_LORA_RANK, -1)], axis=1)
    cos, sin = _rope_tables(seq, tm)
    qn, qp, kcat, v = _pre_mla(
        tok, mod[0], row(norm_mix_w[0]), mla_wq_a[0].astype(BF16), row(mla_q_norm[0]),
        wqb.astype(BF16), wkva.astype(BF16), row(mla_kv_norm[0]), wkvb.astype(BF16), cos, sin,
        nb=nb, tpb=tpb, cpb=cpb)
    o = _attention(qn, qp, kcat, v, nb=nb, tpb=tpb, cpb=cpb)

    moe_w = (exp_w1, exp_w3, exp_w2)
    post_w = lambda i: (row(norm_ffn_w[i]), router_w[i].T, router_bias[i].reshape(-1, 1),
                        shared_w1[i].astype(BF16), shared_w3[i].astype(BF16),
                        shared_w2[i].astype(BF16))
    ptile = POST_TILE if seq % POST_TILE == 0 and (nb * n_ctx) % POST_TILE == 0 else tm
    ppb = seq // ptile
    h2, base, idx8, wt8, rank8, cnt = _post(
        o, mla_wo[0].astype(BF16), jnp.zeros((1, d), F32), tok, mod[0], *post_w(0),
        tm=ptile, nt=(nb * (seq + n_ctx)) // ptile,
        mrow=lambda i: jnp.where(i < nb * ppb, i // ppb, nb), pair_major=True)
    xs = _moe(h2, base, idx8, wt8, rank8, cnt, mod[0], row(final_norm_w), *moe_w,
              layer=0, nt_out=nx + nc, mrow=mrow_all, final_norm=False)

    mrow_x = lambda i: i // tpb
    glu = _pre_conv(xs, mod[1], row(norm_mix_w[1]), conv_pw1_w[0].astype(BF16),
                    row(conv_pw1_b[0]), nt=nx, mrow=mrow_x)
    taps = jnp.concatenate([conv_dw_w[0], jnp.zeros((1, d), F32)], axis=0)
    act = _conv(glu, taps, row(conv_dw_b[0]), row(conv_norm_w[0]), row(conv_norm_b[0]),
                nt=nx, tpb=tpb)
    h2, base, idx8, wt8, rank8, cnt = _post(
        act, conv_pw2_w[0].astype(BF16), row(conv_pw2_b[0]), xs, mod[1],
        *post_w(1), tm=ptile, nt=nb * ppb, mrow=lambda i: i // ppb, pair_major=False)
    out = _moe(h2, base, idx8, wt8, rank8, cnt, mod[1], row(final_norm_w), *moe_w,
               layer=1, nt_out=nx, mrow=mrow_x, final_norm=True)
    return out.reshape(nb, seq, d)
```

```python
import functools

import jax
import jax.numpy as jnp
from jax import lax
from jax.experimental import pallas as pl
from jax.experimental.pallas import tpu as pltpu

N_HEADS = 16
QK_NOPE_DIM = 64
QK_ROPE_DIM = 32
V_HEAD_DIM = 64
Q_LORA_RANK = 512
KV_LORA_RANK = 256
ROPE_THETA = 10000.0
GRID_W = 64
CONV_WIDTH = 31
N_EXPERTS = 64
TOP_K = 6
N_GROUPS = 8
TOPK_GROUPS = 4
ROUTED_SCALE = 2.5
NORM_EPS = 1e-6
LOG2_E = 1.4426950408889634

LANES = 128
SUBLANES = 8
TOKEN_TILE = 256
POST_TILE = 512
EXPERT_ROWS = 512
DISPATCH_TILE = 256
COMBINE_TILE = 256
ISSUE_UNROLL = 4
CONV_HALO = 16
N_PAIRS = N_HEADS // 2
N_ROPE_GROUPS = N_HEADS // 4
VMEM_LIMIT = 56 * 1024 * 1024

F32 = jnp.float32
BF16 = jnp.bfloat16


def _cparams(sem):
    return pltpu.CompilerParams(dimension_semantics=sem, vmem_limit_bytes=VMEM_LIMIT)


def _rms(x, w):
    return x * lax.rsqrt(jnp.mean(x * x, axis=-1, keepdims=True) + NORM_EPS) * w


def _sigmoid(x):
    return 1.0 / (1.0 + jnp.exp(-x))


def _silu(x):
    return x * _sigmoid(x)


def _dot(a, b):
    return jnp.dot(a, b, preferred_element_type=F32)


def _pack_bf16_pairs(x):
    n = x.shape[1] // 2
    hi = lax.bitcast_convert_type(x[:, :n].astype(BF16).astype(F32), jnp.uint32)
    lo = lax.bitcast_convert_type(x[:, n:].astype(BF16).astype(F32), jnp.uint32)
    return hi | (lo >> 16)


def _unpack_bf16_pairs(p):
    hi = lax.bitcast_convert_type(p & jnp.uint32(0xFFFF0000), F32)
    lo = lax.bitcast_convert_type(p << 16, F32)
    return jnp.concatenate([hi, lo], axis=1)


def _store_row_groups(ref, packed):
    m, n = packed.shape
    g = n // LANES
    for c in range(g):
        ref[pl.ds(c, m, stride=g), :] = packed[:, c * LANES:(c + 1) * LANES]


def _load_row_groups(ref, m, g, start=0):
    return jnp.concatenate([ref[pl.ds(start + c, m, stride=g), :] for c in range(g)], axis=1)


def _mod_kernel(c_ref, w_ref, b_ref, o_ref):
    c = c_ref[...]
    o_ref[0] = jnp.dot(_silu(c), w_ref[0], preferred_element_type=F32,
                       precision=lax.Precision.HIGHEST) + b_ref[0]


def _modulation(cvec, mod_w, mod_b):
    depth, d, n = mod_w.shape
    rows = cvec.shape[0]
    tn = 1536
    return pl.pallas_call(
        _mod_kernel,
        grid=(depth, n // tn),
        in_specs=[pl.BlockSpec((rows, d), lambda l, j: (0, 0)),
                  pl.BlockSpec((1, d, tn), lambda l, j: (l, 0, j)),
                  pl.BlockSpec((1, 1, tn), lambda l, j: (l, 0, j))],
        out_specs=pl.BlockSpec((1, rows, tn), lambda l, j: (l, 0, j)),
        out_shape=jax.ShapeDtypeStruct((depth, rows, n), F32),
        compiler_params=_cparams(("arbitrary", "arbitrary")),
        name="modulation",
    )(cvec, mod_w, mod_b.reshape(depth, 1, n))


def _rope(v, cos, sin):
    n = v.shape[1]
    lane = lax.broadcasted_iota(jnp.int32, v.shape, 1)
    up = pltpu.roll(v, 8, axis=1)
    dn = pltpu.roll(v, n - 8, axis=1)
    rot = jnp.where(lane % 16 < 8, -dn, up)
    return v * cos + rot * sin


def _pre_mla_kernel(lat_ref, ctx_ref, mod_ref, nw_ref, wqa_ref, qn_ref, wqb_ref, wkva_ref, kvn_ref,
                    wkvb_ref, cos_ref, sin_ref, qn_out, qp_out, k_out, v_out, *, n_lat_tiles):
    x = jnp.where(pl.program_id(0) < n_lat_tiles, lat_ref[...], ctx_ref[...])
    m = mod_ref[0]
    h = _rms(x, nw_ref[...]) * (1.0 + m[1:2]) + m[0:1]
    hb = h.astype(BF16)
    cos = cos_ref[...]
    sin = sin_ref[...]

    qa = _rms(_dot(hb, wqa_ref[...]), qn_ref[...])
    scale = (QK_NOPE_DIM + QK_ROPE_DIM) ** -0.5 * LOG2_E
    q = _dot(qa.astype(BF16), wqb_ref[...]) * scale
    d_nope = N_HEADS * QK_NOPE_DIM
    q_pe = _rope(q[:, d_nope:], cos, sin)

    kva = _dot(hb, wkva_ref[...])
    kpe4_t = _rope(kva[:, KV_LORA_RANK:], cos[:, :LANES], sin[:, :LANES]).T.astype(BF16)
    ckv = _rms(kva[:, :KV_LORA_RANK], kvn_ref[...])
    kv = _dot(ckv.astype(BF16), wkvb_ref[...])

    for j in range(N_PAIRS):
        sl = slice(j * LANES, (j + 1) * LANES)
        qn_out[j] = q[:, sl].astype(BF16)
        k_out[j, :LANES, :] = kv[:, sl].T.astype(BF16)
        k_out[j, LANES:, :] = kpe4_t
        v_out[j] = kv[:, d_nope + j * LANES:d_nope + (j + 1) * LANES].astype(BF16)
    for g in range(N_ROPE_GROUPS):
        qp_out[g] = q_pe[:, g * LANES:(g + 1) * LANES].astype(BF16)


def _pre_mla(lat, ctx, mod, nw, wqa, qn, wqb, wkva, kvn, wkvb, cos, sin, *, nb, tpb, cpb):
    d = lat.shape[1]
    t = lat.shape[0] + ctx.shape[0]
    tm = TOKEN_TILE
    nx = nb * tpb
    spb = tpb + cpb

    def mrow(i):
        return jnp.where(i < nx, i // tpb, nb)

    def kvblk(i):
        ic = i - nx
        return jnp.where(i < nx, (i // tpb) * spb + cpb + i % tpb, (ic // cpb) * spb + ic % cpb)

    def ropeblk(i):
        return jnp.where(i < nx, i % tpb, tpb)

    full = lambda a: pl.BlockSpec(a.shape, lambda i: (0,) * a.ndim)
    s_rows = nb * spb * tm
    return pl.pallas_call(
        functools.partial(_pre_mla_kernel, n_lat_tiles=nx),
        grid=(t // tm,),
        in_specs=[pl.BlockSpec((tm, d), lambda i: (jnp.minimum(i, nx - 1), 0)),
                  pl.BlockSpec((tm, d), lambda i: (jnp.maximum(i - nx, 0), 0)),
                  pl.BlockSpec((1, 6, d), lambda i: (mrow(i), 0, 0)),
                  full(nw), full(wqa), full(qn), full(wqb), full(wkva), full(kvn), full(wkvb),
                  pl.BlockSpec((tm, cos.shape[1]), lambda i: (ropeblk(i), 0)),
                  pl.BlockSpec((tm, sin.shape[1]), lambda i: (ropeblk(i), 0))],
        out_specs=[pl.BlockSpec((N_PAIRS, tm, LANES), lambda i: (0, i, 0)),
                   pl.BlockSpec((N_ROPE_GROUPS, tm, LANES), lambda i: (0, i, 0)),
                   pl.BlockSpec((N_PAIRS, 2 * LANES, tm), lambda i: (0, 0, kvblk(i))),
                   pl.BlockSpec((N_PAIRS, tm, LANES), lambda i: (0, kvblk(i), 0))],
        out_shape=[jax.ShapeDtypeStruct((N_PAIRS, t, LANES), BF16),
                   jax.ShapeDtypeStruct((N_ROPE_GROUPS, t, LANES), BF16),
                   jax.ShapeDtypeStruct((N_PAIRS, 2 * LANES, s_rows), BF16),
                   jax.ShapeDtypeStruct((N_PAIRS, s_rows, LANES), BF16)],
        compiler_params=_cparams(("arbitrary",)),
        name="pre_mla",
    )(lat, ctx, mod, nw, wqa, qn, wqb, wkva, kvn, wkvb, cos, sin)


def _attn_tile(qn_ref, qp_ref, k_ref, v_ref, o_ref, s_len):
    tq = qn_ref.shape[1]
    lane = lax.broadcasted_iota(jnp.int32, (tq, LANES), 1)

    def pair(j, carry):
        qn = qn_ref[j]
        qp = qp_ref[j // 2]
        k = k_ref[j, :, :s_len]
        v = v_ref[j, :s_len, :]
        outs = []
        for e in range(2):
            qn_m = jnp.where(lane // QK_NOPE_DIM == e, qn, jnp.zeros_like(qn))
            qp_m = jnp.where(lane // QK_ROPE_DIM == 2 * (j % 2) + e, qp, jnp.zeros_like(qp))
            lhs = jnp.concatenate([qn_m, qp_m], axis=1)
            s = _dot(lhs, k)
            p = jnp.exp2(s - jnp.max(s, axis=-1, keepdims=True))
            l = jnp.sum(p, axis=-1, keepdims=True)
            outs.append(_dot(p.astype(BF16), v) * (1.0 / l))
        o_ref[j] = jnp.where(lane < V_HEAD_DIM, outs[0], outs[1]).astype(BF16)
        return carry

    lax.fori_loop(0, N_PAIRS, pair, 0, unroll=8)


def _attn_kernel(qn_ref, qp_ref, k_ref, v_ref, o_ref, *, tpb, n_ctx):
    step = pl.program_id(1)

    @pl.when(step < tpb)
    def _():
        _attn_tile(qn_ref, qp_ref, k_ref, v_ref, o_ref, v_ref.shape[1])

    @pl.when(step >= tpb)
    def _():
        _attn_tile(qn_ref, qp_ref, k_ref, v_ref, o_ref, n_ctx)


def _attention(qn, qp, kcat, v, *, nb, tpb, cpb):
    t = qn.shape[1]
    tq = TOKEN_TILE
    nx = nb * tpb
    s_len = (tpb + cpb) * tq

    def qblk(b, i):
        return jnp.where(i < tpb, b * tpb + i, nx + b * cpb + (i - tpb))

    return pl.pallas_call(
        functools.partial(_attn_kernel, tpb=tpb, n_ctx=cpb * tq),
        grid=(nb, tpb + cpb),
        in_specs=[pl.BlockSpec((N_PAIRS, tq, LANES), lambda b, i: (0, qblk(b, i), 0)),
                  pl.BlockSpec((N_ROPE_GROUPS, tq, LANES), lambda b, i: (0, qblk(b, i), 0)),
                  pl.BlockSpec((N_PAIRS, 2 * LANES, s_len), lambda b, i: (0, 0, b)),
                  pl.BlockSpec((N_PAIRS, s_len, LANES), lambda b, i: (0, b, 0))],
        out_specs=pl.BlockSpec((N_PAIRS, tq, LANES), lambda b, i: (0, qblk(b, i), 0)),
        out_shape=jax.ShapeDtypeStruct((N_PAIRS, t, LANES), BF16),
        compiler_params=_cparams(("arbitrary", "arbitrary")),
        name="attention",
    )(qn, qp, kcat, v)


def _post_kernel(act_ref, wmix_ref, bmix_ref, xs_ref, xs2_ref, mod_ref, nfw_ref, rwt_ref, rb_ref,
                 sw1_ref, sw3_ref, sw2_ref,
                 h2_out, base_out, idx_out, wt_out, rank_out, cnt_out, cnt_scr,
                 *, pair_major, n_first):
    i = pl.program_id(0)
    tm = xs_ref.shape[0]
    xs_in = jnp.where(i < n_first, xs_ref[...], xs2_ref[...])

    @pl.when(i == 0)
    def _():
        cnt_scr[...] = jnp.zeros_like(cnt_scr)

    if pair_major:
        act = jnp.concatenate([act_ref[j] for j in range(N_PAIRS)], axis=1)
    else:
        act = act_ref[...]
    m = mod_ref[0]
    mix = _dot(act, wmix_ref[...]) + bmix_ref[...]
    xs = xs_in + m[2:3] * mix
    h2 = _rms(xs, nfw_ref[...]) * (1.0 + m[4:5]) + m[3:4]
    _store_row_groups(h2_out, _pack_bf16_pairs(h2))

    h2b = h2.astype(BF16)
    hid = _silu(_dot(h2b, sw1_ref[...])) * _dot(h2b, sw3_ref[...])
    base_out[...] = xs + m[5:6] * _dot(hid.astype(BF16), sw2_ref[...])

    logits = lax.dot_general(rwt_ref[...], h2, (((1,), (1,)), ((), ())),
                             preferred_element_type=F32, precision=lax.Precision.HIGHEST)
    scores = _sigmoid(logits)
    sel = scores + rb_ref[...]
    per_group = N_EXPERTS // N_GROUPS
    neg = jnp.float32(-jnp.inf)
    sub_iota = lax.broadcasted_iota(jnp.int32, (per_group, tm), 0)
    gs_rows = []
    for g in range(N_GROUPS):
        sg = sel[g * per_group:(g + 1) * per_group, :]
        m1 = jnp.max(sg, axis=0, keepdims=True)
        first = jnp.min(jnp.where(sg == m1, sub_iota, per_group), axis=0, keepdims=True)
        m2 = jnp.max(jnp.where(sub_iota == first, neg, sg), axis=0, keepdims=True)
        gs_rows.append(m1 + m2)
    gs = jnp.concatenate(gs_rows, axis=0)
    g_iota = lax.broadcasted_iota(jnp.int32, (N_GROUPS, tm), 0)
    gmask = jnp.zeros((N_GROUPS, tm), F32)
    work = gs
    for _ in range(TOPK_GROUPS):
        mx = jnp.max(work, axis=0, keepdims=True)
        gi = jnp.min(jnp.where(work == mx, g_iota, N_GROUPS), axis=0, keepdims=True)
        pick = g_iota == gi
        gmask = jnp.where(pick, 1.0, gmask)
        work = jnp.where(pick, neg, work)
    emask = jnp.concatenate(
        [jnp.broadcast_to(gmask[g:g + 1, :], (per_group, tm)) for g in range(N_GROUPS)], axis=0)
    masked = jnp.where(emask > 0.5, sel, neg)
    e_iota = lax.broadcasted_iota(jnp.int32, (N_EXPERTS, tm), 0)
    picks, idx_rows, w_rows = [], [], []
    for _ in range(TOP_K):
        mx = jnp.max(masked, axis=0, keepdims=True)
        ei = jnp.min(jnp.where(masked == mx, e_iota, N_EXPERTS), axis=0, keepdims=True)
        pick = e_iota == ei
        picks.append(pick)
        idx_rows.append(ei)
        w_rows.append(jnp.sum(jnp.where(pick, scores, 0.0), axis=0, keepdims=True))
        masked = jnp.where(pick, neg, masked)
    wsum = w_rows[0]
    for r in w_rows[1:]:
        wsum = wsum + r
    wnorm = ROUTED_SCALE / wsum

    member = jnp.zeros((N_EXPERTS, tm), F32)
    for pick in picks:
        member = jnp.where(pick, 1.0, member)
    before = (lax.broadcasted_iota(jnp.int32, (tm, tm), 0)
              < lax.broadcasted_iota(jnp.int32, (tm, tm), 1))
    prefix = _dot(member.astype(BF16), jnp.where(before, 1.0, 0.0).astype(BF16))
    cnt = cnt_scr[...]
    rank_full = cnt[:, 0:1] + prefix
    rank_rows = [jnp.sum(jnp.where(pick, rank_full, 0.0), axis=0, keepdims=True)
                 for pick in picks]
    cnt = cnt + jnp.sum(member, axis=1, keepdims=True)
    cnt_scr[...] = cnt
    cnt_out[...] = cnt

    pad = SUBLANES - TOP_K
    zi = jnp.zeros((pad, tm), jnp.int32)
    zf = jnp.zeros((pad, tm), F32)
    idx_out[...] = jnp.concatenate(idx_rows + [zi], axis=0)
    wt_out[...] = jnp.concatenate([r * wnorm for r in w_rows] + [zf], axis=0)
    rank_out[...] = jnp.concatenate([r.astype(jnp.int32) for r in rank_rows] + [zi], axis=0)


def _post(act, wmix, bmix, xs, xs2, mod, nfw, rwt, rb, sw1, sw3, sw2, *, tm, nt, mrow,
          pair_major):
    d = xs.shape[1]
    t = nt * tm
    g = d // 2 // LANES
    n_first = min(nt, xs.shape[0] // tm)
    full = lambda a: pl.BlockSpec(a.shape, lambda i: (0,) * a.ndim)
    if pair_major:
        act_spec = pl.BlockSpec((N_PAIRS, tm, LANES), lambda i: (0, i, 0))
    else:
        act_spec = pl.BlockSpec((tm, act.shape[1]), lambda i: (i, 0))
    row8 = pl.BlockSpec((SUBLANES, tm), lambda i: (0, i))
    return pl.pallas_call(
        functools.partial(_post_kernel, pair_major=pair_major, n_first=n_first),
        grid=(nt,),
        in_specs=[act_spec, full(wmix), full(bmix),
                  pl.BlockSpec((tm, d), lambda i: (jnp.minimum(i, n_first - 1), 0)),
                  pl.BlockSpec((tm, d), lambda i: (jnp.maximum(i - n_first, 0), 0)),
                  pl.BlockSpec((1, 6, d), lambda i: (mrow(i), 0, 0)),
                  full(nfw), full(rwt), full(rb), full(sw1), full(sw3), full(sw2)],
        out_specs=[pl.BlockSpec((tm * g, LANES), lambda i: (i, 0)),
                   pl.BlockSpec((tm, d), lambda i: (i, 0)),
                   row8, row8, row8,
                   pl.BlockSpec((N_EXPERTS, LANES), lambda i: (0, 0))],
        out_shape=[jax.ShapeDtypeStruct((t * g, LANES), jnp.uint32),
                   jax.ShapeDtypeStruct((t, d), F32),
                   jax.ShapeDtypeStruct((SUBLANES, t), jnp.int32),
                   jax.ShapeDtypeStruct((SUBLANES, t), F32),
                   jax.ShapeDtypeStruct((SUBLANES, t), jnp.int32),
                   jax.ShapeDtypeStruct((N_EXPERTS, LANES), F32)],
        scratch_shapes=[pltpu.VMEM((N_EXPERTS, LANES), F32)],
        compiler_params=_cparams(("arbitrary",)),
        name="post_mixer",
    )(act, wmix, bmix, xs, xs2, mod, nfw, rwt, rb, sw1, sw3, sw2)


def _dispatch_kernel(dest_ref, fill_ref, h2_ref, xg_out, zbuf, sem, zsem, *, t_total, tile, g):
    step = pl.program_id(0)
    base = step * tile
    rows = zbuf.shape[0]
    n_blocks = xg_out.shape[0] // rows

    @pl.when(step == 0)
    def _():
        zbuf[...] = jnp.zeros_like(zbuf)

        def fill_copy(n):
            return pltpu.make_async_copy(
                zbuf, xg_out.at[pl.ds(pl.multiple_of(n * rows, rows), rows)], zsem)

        def start(n, carry):
            @pl.when(fill_ref[n] > 0)
            def _():
                fill_copy(n).start()
            return carry

        def wait(n, carry):
            @pl.when(fill_ref[n] > 0)
            def _():
                fill_copy(n).wait()
            return carry

        lax.fori_loop(0, n_blocks, start, 0)
        lax.fori_loop(0, n_blocks, wait, 0)

    def row_copy(r, slot):
        return pltpu.make_async_copy(h2_ref.at[pl.ds(pl.multiple_of(r * g, g), g)],
                                     xg_out.at[pl.ds(pl.multiple_of(slot * g, g), g)], sem)

    def issue(r, carry):
        for k in range(TOP_K):
            row_copy(r, dest_ref[k * t_total + base + r]).start(priority=k % 2)
        return carry

    lax.fori_loop(0, tile, issue, 0, unroll=ISSUE_UNROLL)

    def drain(r, carry):
        for k in range(TOP_K):
            row_copy(0, 0).wait()
        return carry

    lax.fori_loop(0, tile, drain, 0, unroll=ISSUE_UNROLL)


def _dispatch(dest, fill, h2, n_slots, g):
    t = h2.shape[0] // g
    tile = DISPATCH_TILE
    return pl.pallas_call(
        functools.partial(_dispatch_kernel, t_total=t, tile=tile, g=g),
        grid_spec=pltpu.PrefetchScalarGridSpec(
            num_scalar_prefetch=2,
            grid=(t // tile,),
            in_specs=[pl.BlockSpec((tile * g, LANES), lambda i, dr, fr: (i, 0))],
            out_specs=pl.BlockSpec(memory_space=pl.ANY),
            scratch_shapes=[pltpu.VMEM((EXPERT_ROWS * g, LANES), h2.dtype),
                            pltpu.SemaphoreType.DMA(()), pltpu.SemaphoreType.DMA(())]),
        out_shape=jax.ShapeDtypeStruct((n_slots * g, LANES), h2.dtype),
        compiler_params=_cparams(("arbitrary",)),
        name="moe_dispatch",
    )(dest, fill, h2)


def _expert_kernel(be_ref, nu_ref, x_ref, w1_ref, w3_ref, w2_ref, y_ref, w1b, w3b, w2b, *, g):
    n = pl.program_id(0)
    used = n < nu_ref[0]
    rows = x_ref.shape[0] // g

    @pl.when(jnp.logical_or(n == 0, be_ref[n] != be_ref[jnp.maximum(n - 1, 0)]))
    def _():
        w1b[...] = w1_ref[0, 0].astype(BF16)
        w3b[...] = w3_ref[0, 0].astype(BF16)
        w2b[...] = w2_ref[0, 0].astype(BF16)

    @pl.when(used)
    def _():
        x = _unpack_bf16_pairs(_load_row_groups(x_ref, rows, g)).astype(BF16)
        hid = _silu(_dot(x, w1b[...])) * _dot(x, w3b[...])
        _store_row_groups(y_ref, _pack_bf16_pairs(_dot(hid.astype(BF16), w2b[...])))

    @pl.when(jnp.logical_not(used))
    def _():
        y_ref[...] = jnp.zeros_like(y_ref)


def _experts(blk_e, n_used, xg, w1, w3, w2, g, layer):
    rows = EXPERT_ROWS * g
    nblk = xg.shape[0] // rows
    _, _, d, e_dim = w1.shape

    def xmap(n, be, nu):
        return (jnp.minimum(n, nu[0] - 1), 0)

    def wmap(n, be, nu):
        return (layer, be[n], 0, 0)

    return pl.pallas_call(
        functools.partial(_expert_kernel, g=g),
        grid_spec=pltpu.PrefetchScalarGridSpec(
            num_scalar_prefetch=2,
            grid=(nblk,),
            in_specs=[pl.BlockSpec((rows, LANES), xmap),
                      pl.BlockSpec((1, 1, d, e_dim), wmap),
                      pl.BlockSpec((1, 1, d, e_dim), wmap),
                      pl.BlockSpec((1, 1, e_dim, d), wmap)],
            out_specs=pl.BlockSpec((rows, LANES), lambda n, be, nu: (n, 0)),
            scratch_shapes=[pltpu.VMEM((d, e_dim), BF16), pltpu.VMEM((d, e_dim), BF16),
                            pltpu.VMEM((e_dim, d), BF16)]),
        out_shape=jax.ShapeDtypeStruct(xg.shape, jnp.uint32),
        compiler_params=_cparams(("arbitrary",)),
        name="moe_experts",
    )(blk_e, n_used, xg, w1, w3, w2)


def _combine_kernel(dest_ref, y_hbm, base_ref, wt_ref, mod_ref, fnw_ref, o_ref, buf, sems,
                    *, t_total, n_tiles, final_norm, g):
    tm = base_ref.shape[0]
    step = pl.program_id(0)
    cur = step % 2
    half = TOP_K * tm * g
    chunk = SUBLANES
    gate = mod_ref[0][5:6]
    fnw = fnw_ref[...]

    def row_copy(slot, which, k, r):
        dst = pl.multiple_of(which * half + (k * tm + r) * g, g)
        return pltpu.make_async_copy(y_hbm.at[pl.ds(pl.multiple_of(slot * g, g), g)],
                                     buf.at[pl.ds(dst, g)], sems.at[which])

    def issue_chunk(tile, which, r0):
        for dr in range(chunk):
            for k in range(TOP_K):
                slot = dest_ref[k * t_total + tile * tm + r0 + dr]
                row_copy(slot, which, k, r0 + dr).start(priority=k % 2)

    def sum_chunk(r0):
        wt = wt_ref[pl.ds(r0, chunk), :]
        acc = None
        for k in range(TOP_K):
            start = cur * half + (k * tm + r0) * g
            packed = _load_row_groups(buf, chunk, g, start=start)
            term = wt[:, k:k + 1] * _unpack_bf16_pairs(packed)
            acc = term if acc is None else acc + term
        out = base_ref[pl.ds(r0, chunk), :] + gate * acc
        if final_norm:
            out = _rms(out, fnw)
        o_ref[pl.ds(r0, chunk), :] = out

    @pl.when(step == 0)
    def _():
        def first(c, carry):
            issue_chunk(0, 0, pl.multiple_of(c * chunk, chunk))
            return carry
        lax.fori_loop(0, tm // chunk, first, 0)

    def drain(r, carry):
        for k in range(TOP_K):
            row_copy(0, cur, k, 0).wait()
        return carry

    lax.fori_loop(0, tm, drain, 0, unroll=ISSUE_UNROLL)

    @pl.when(step + 1 < n_tiles)
    def _():
        def body(c, carry):
            r0 = pl.multiple_of(c * chunk, chunk)
            sum_chunk(r0)
            issue_chunk(step + 1, 1 - cur, r0)
            return carry
        lax.fori_loop(0, tm // chunk, body, 0, unroll=2)

    @pl.when(step + 1 == n_tiles)
    def _():
        def body(c, carry):
            sum_chunk(pl.multiple_of(c * chunk, chunk))
            return carry
        lax.fori_loop(0, tm // chunk, body, 0)


def _combine(dest, yg, base, wt_t, mod, fnw, *, nt, mrow, final_norm, g):
    d = base.shape[1]
    tm = COMBINE_TILE
    t = nt * tm
    return pl.pallas_call(
        functools.partial(_combine_kernel, t_total=base.shape[0], n_tiles=nt,
                          final_norm=final_norm, g=g),
        grid_spec=pltpu.PrefetchScalarGridSpec(
            num_scalar_prefetch=1,
            grid=(nt,),
            in_specs=[pl.BlockSpec(memory_space=pl.ANY),
                      pl.BlockSpec((tm, d), lambda i, dr: (i, 0)),
                      pl.BlockSpec((tm, SUBLANES), lambda i, dr: (i, 0)),
                      pl.BlockSpec((1, 6, d), lambda i, dr: (mrow(i), 0, 0)),
                      pl.BlockSpec((1, d), lambda i, dr: (0, 0))],
            out_specs=pl.BlockSpec((tm, d), lambda i, dr: (i, 0)),
            scratch_shapes=[pltpu.VMEM((2 * TOP_K * tm * g, LANES), jnp.uint32),
                            pltpu.SemaphoreType.DMA((2,))]),
        out_shape=jax.ShapeDtypeStruct((t, d), F32),
        compiler_params=_cparams(("arbitrary",)),
        name="moe_combine",
    )(dest, yg, base, wt_t, mod, fnw)


def _moe(h2, base, idx8, wt8, rank8, cnt, mod, fnw, w1, w3, w2, *, layer, nt_out, mrow,
         final_norm):
    g = base.shape[1] // 2 // LANES
    t = h2.shape[0] // g
    rows = EXPERT_ROWS
    counts = cnt[:, 0].astype(jnp.int32)
    padded = (counts + rows - 1) // rows * rows
    pad_end = jnp.cumsum(padded)
    pad_start = pad_end - padded
    nblk = (t * TOP_K + rows - 1) // rows + N_EXPERTS
    e_ids = jnp.arange(N_EXPERTS, dtype=jnp.int32)
    idx = idx8[:TOP_K]
    start_of = jnp.sum(jnp.where(idx[:, :, None] == e_ids, pad_start, 0), axis=-1)
    dest = (start_of + rank8[:TOP_K]).reshape(-1)
    n_used = (pad_end[-1] // rows).astype(jnp.int32)
    blk_start = jnp.arange(nblk, dtype=jnp.int32) * rows
    e_of_blk = jnp.sum((pad_end[None, :] <= blk_start[:, None]).astype(jnp.int32), axis=1)
    valid_end = jnp.sum(jnp.where(e_of_blk[:, None] == e_ids, pad_start + counts, 0), axis=1)
    fill = (blk_start + rows > valid_end).astype(jnp.int32)
    e_last = jnp.max(jnp.where(counts > 0, e_ids, 0))
    blk_e = jnp.minimum(e_of_blk, e_last).astype(jnp.int32)
    xg = _dispatch(dest, fill, h2, nblk * rows, g)
    yg = _experts(blk_e, n_used.reshape(1), xg, w1, w3, w2, g, layer)
    tile_ratio = TOKEN_TILE // COMBINE_TILE
    return _combine(dest, yg, base, wt8.T, mod, fnw, nt=nt_out * tile_ratio,
                    mrow=lambda i: mrow(i // tile_ratio), final_norm=final_norm, g=g)


def _pre_conv_kernel(xs_ref, mod_ref, nw_ref, w_ref, b_ref, o_ref):
    m = mod_ref[0]
    h = _rms(xs_ref[...], nw_ref[...]) * (1.0 + m[1:2]) + m[0:1]
    u = _dot(h.astype(BF16), w_ref[...]) + b_ref[...]
    d = o_ref.shape[1]
    o_ref[...] = u[:, :d] * _sigmoid(u[:, d:])


def _pre_conv(xs, mod, nw, w, b, *, nt, mrow):
    d = xs.shape[1]
    tm = TOKEN_TILE
    full = lambda a: pl.BlockSpec(a.shape, lambda i: (0,) * a.ndim)
    return pl.pallas_call(
        _pre_conv_kernel,
        grid=(nt,),
        in_specs=[pl.BlockSpec((tm, d), lambda i: (i, 0)),
                  pl.BlockSpec((1, 6, d), lambda i: (mrow(i), 0, 0)),
                  full(nw), full(w), full(b)],
        out_specs=pl.BlockSpec((tm, d), lambda i: (i, 0)),
        out_shape=jax.ShapeDtypeStruct((nt * tm, d), F32),
        compiler_params=_cparams(("arbitrary",)),
        name="pre_conv",
    )(xs, mod, nw, w, b)


def _conv_kernel(prev_ref, cur_ref, next_ref, dw_ref, db_ref, lw_ref, lb_ref, o_ref, win, conv,
                 *, tpb):
    tm, d = cur_ref.shape
    r = pl.program_id(0) % tpb
    halo = CONV_HALO
    pad = CONV_WIDTH // 2
    win[0:halo, :] = jnp.where(r > 0, prev_ref[...], 0.0)
    win[halo:halo + tm, :] = cur_ref[...]
    win[halo + tm:halo + tm + halo, :] = jnp.where(r < tpb - 1, next_ref[...], 0.0)

    rows = 64
    for c in range(d // LANES):
        cs = slice(c * LANES, (c + 1) * LANES)
        taps = dw_ref[:, cs]
        bias = db_ref[:, cs]

        def chunk(q, carry):
            r0 = pl.multiple_of(q * rows, rows)
            slab = win[pl.ds(r0, rows + 2 * halo), cs]
            acc = jnp.zeros((rows, LANES), F32) + bias
            for b in range(SUBLANES):
                part = None
                for j in range(CONV_WIDTH):
                    off = halo - pad + j
                    if off % SUBLANES == b:
                        a = off - b
                        term = taps[j:j + 1, :] * slab[a:a + rows + SUBLANES, :]
                        part = term if part is None else part + term
                acc = acc + part[b:b + rows, :]
            conv[pl.ds(r0, rows), cs] = acc
            return carry

        lax.fori_loop(0, tm // rows, chunk, 0)

    u = conv[...]
    mu = jnp.mean(u, axis=-1, keepdims=True)
    var = jnp.mean(jnp.square(u - mu), axis=-1, keepdims=True)
    y = (u - mu) * lax.rsqrt(var + NORM_EPS) * lw_ref[...] + lb_ref[...]
    o_ref[...] = _silu(y).astype(BF16)


def _conv(glu, dw, db, lw, lb, *, nt, tpb):
    d = glu.shape[1]
    tm = TOKEN_TILE
    hb = tm // CONV_HALO
    n_halo_blocks = glu.shape[0] // CONV_HALO
    full = lambda a: pl.BlockSpec(a.shape, lambda i: (0,) * a.ndim)
    return pl.pallas_call(
        functools.partial(_conv_kernel, tpb=tpb),
        grid=(nt,),
        in_specs=[pl.BlockSpec((CONV_HALO, d), lambda i: (jnp.maximum(i * hb - 1, 0), 0)),
                  pl.BlockSpec((tm, d), lambda i: (i, 0)),
                  pl.BlockSpec((CONV_HALO, d),
                               lambda i: (jnp.minimum((i + 1) * hb, n_halo_blocks - 1), 0)),
                  full(dw), full(db), full(lw), full(lb)],
        out_specs=pl.BlockSpec((tm, d), lambda i: (i, 0)),
        out_shape=jax.ShapeDtypeStruct((nt * tm, d), BF16),
        scratch_shapes=[pltpu.VMEM((tm + 2 * CONV_HALO, d), F32), pltpu.VMEM((tm, d), F32)],
        compiler_params=_cparams(("arbitrary",)),
        name="dw_conv",
    )(glu, glu, glu, dw, db, lw, lb)


def _rope_tables(n_tokens, extra_rows):
    rows = n_tokens // GRID_W
    row = jnp.repeat(jnp.arange(rows, dtype=F32), GRID_W)
    col = jnp.tile(jnp.arange(GRID_W, dtype=F32), rows)
    axis_dim = QK_ROPE_DIM // 2
    inv_freq = ROPE_THETA ** (-jnp.arange(0, axis_dim, 2, dtype=F32) / axis_dim)
    ang_r = row[:, None] * inv_freq
    ang_c = col[:, None] * inv_freq
    ang = jnp.concatenate([ang_r, ang_r, ang_c, ang_c], axis=-1)
    cos = jnp.concatenate([jnp.cos(ang), jnp.ones((extra_rows, QK_ROPE_DIM), F32)], axis=0)
    sin = jnp.concatenate([jnp.sin(ang), jnp.zeros((extra_rows, QK_ROPE_DIM), F32)], axis=0)
    return jnp.tile(cos, (1, N_HEADS)), jnp.tile(sin, (1, N_HEADS))


def kernel(x, c, ctx, c_ctx, mod_w, mod_b, norm_mix_w, norm_ffn_w, mla_wq_a, mla_q_norm, mla_wq_b, mla_wkv_a, mla_kv_norm, mla_wkv_b, mla_wo, conv_pw1_w, conv_pw1_b, conv_dw_w, conv_dw_b, conv_norm_w, conv_norm_b, conv_pw2_w, conv_pw2_b, router_w, router_bias, exp_w1, exp_w3, exp_w2, shared_w1, shared_w3, shared_w2, final_norm_w):
    nb, seq, d = x.shape
    n_ctx = ctx.shape[1]
    tm = TOKEN_TILE
    tpb, cpb = seq // tm, n_ctx // tm
    nx, nc = nb * tpb, nb * cpb
    row = lambda a: a.reshape(1, -1)

    pad_rows = (-(nb + 1)) % SUBLANES
    cvec = jnp.concatenate([c, c_ctx[None, :], jnp.zeros((pad_rows, d), F32)], axis=0)
    mod = _modulation(cvec, mod_w, mod_b).reshape(mod_w.shape[0], cvec.shape[0], 6, d)

    def mrow_all(i):
        return jnp.where(i < nx, i // tpb, nb)

    lat2d, ctx2d = x.reshape(nb * seq, d), ctx.reshape(nb * n_ctx, d)
    wqb =mla_wq_b[0].reshape(Q_LORA_RANK, N_HEADS, QK_NOPE_DIM + QK_ROPE_DIM)
    wqb = jnp.concatenate([wqb[:, :, :QK_NOPE_DIM].reshape(Q_LORA_RANK, -1),
                           wqb[:, :, QK_NOPE_DIM:].reshape(Q_LORA_RANK, -1)], axis=1)
    wkva = mla_wkv_a[0]
    wkva = jnp.concatenate([wkva[:, :KV_LORA_RANK]]
                           + [wkva[:, KV_LORA_RANK:]] * (LANES // QK_ROPE_DIM), axis=1)
    wkvb = mla_wkv_b[0].reshape(KV_LORA_RANK, N_HEADS, QK_NOPE_DIM + V_HEAD_DIM)
    wkvb = jnp.concatenate([wkvb[:, :, :QK_NOPE_DIM].reshape(KV_LORA_RANK, -1),
                            wkvb[:, :, QK_NOPE_DIM:].reshape(KV_LORA_RANK, -1)], axis=1)
    cos, sin = _rope_tables(seq, tm)
    qn, qp, kcat, v = _pre_mla(
        lat2d, ctx2d, mod[0], row(norm_mix_w[0]), mla_wq_a[0].astype(BF16), row(mla_q_norm[0]),
        wqb.astype(BF16), wkva.astype(BF16), row(mla_kv_norm[0]), wkvb.astype(BF16), cos, sin,
        nb=nb, tpb=tpb, cpb=cpb)
    o = _attention(qn, qp, kcat, v, nb=nb, tpb=tpb, cpb=cpb)

    moe_w = (exp_w1, exp_w3, exp_w2)
    post_w = lambda i: (row(norm_ffn_w[i]), router_w[i].T, router_bias[i].reshape(-1, 1),
                        shared_w1[i].astype(BF16), shared_w3[i].astype(BF16),
                        shared_w2[i].astype(BF16))
    ptile = POST_TILE if seq % POST_TILE == 0 and (nb * n_ctx) % POST_TILE == 0 else tm
    ppb = seq // ptile
    h2, base, idx8, wt8, rank8, cnt = _post(
        o, mla_wo[0].astype(BF16), jnp.zeros((1, d), F32), lat2d, ctx2d, mod[0], *post_w(0),
        tm=ptile, nt=(nb * (seq + n_ctx)) // ptile,
        mrow=lambda i: jnp.where(i < nb * ppb, i // ppb, nb), pair_major=True)
    xs = _moe(h2, base, idx8, wt8, rank8, cnt, mod[0], row(final_norm_w), *moe_w,
              layer=0, nt_out=nx + nc, mrow=mrow_all, final_norm=False)

    mrow_x = lambda i: i // tpb
    glu = _pre_conv(xs, mod[1], row(norm_mix_w[1]), conv_pw1_w[0].astype(BF16),
                    row(conv_pw1_b[0]), nt=nx, mrow=mrow_x)
    taps = jnp.concatenate([conv_dw_w[0], jnp.zeros((1, d), F32)], axis=0)
    act = _conv(glu, taps, row(conv_dw_b[0]), row(conv_norm_w[0]), row(conv_norm_b[0]),
                nt=nx, tpb=tpb)
    h2, base, idx8, wt8, rank8, cnt = _post(
        act, conv_pw2_w[0].astype(BF16), row(conv_pw2_b[0]), xs, xs, mod[1],
        *post_w(1), tm=ptile, nt=nb * ppb, mrow=lambda i: i // ppb, pair_major=False)
    out = _moe(h2, base, idx8, wt8, rank8, cnt, mod[1], row(final_norm_w), *moe_w,
               layer=1, nt_out=nx, mrow=mrow_x, final_norm=True)
    return out.reshape(nb, seq, d)
```

```python
import functools

import jax
import jax.numpy as jnp
from jax import lax
from jax.experimental import pallas as pl
from jax.experimental.pallas import tpu as pltpu

N_HEADS = 16
QK_NOPE_DIM = 64
QK_ROPE_DIM = 32
V_HEAD_DIM = 64
Q_LORA_RANK = 512
KV_LORA_RANK = 256
ROPE_THETA = 10000.0
GRID_W = 64
CONV_WIDTH = 31
N_EXPERTS = 64
TOP_K = 6
N_GROUPS = 8
TOPK_GROUPS = 4
ROUTED_SCALE = 2.5
NORM_EPS = 1e-6
LOG2_E = 1.4426950408889634

LANES = 128
SUBLANES = 8
TOKEN_TILE = 256
POST_TILE = 512
EXPERT_ROWS = 512
DISPATCH_TILE = 256
COMBINE_TILE = 256
ISSUE_UNROLL = 4
CONV_HALO = 16
N_PAIRS = N_HEADS // 2
N_ROPE_GROUPS = N_HEADS // 4
VMEM_LIMIT = 56 * 1024 * 1024

F32 = jnp.float32
BF16 = jnp.bfloat16


def _cparams(sem):
    return pltpu.CompilerParams(dimension_semantics=sem, vmem_limit_bytes=VMEM_LIMIT)


def _rms(x, w):
    return x * lax.rsqrt(jnp.mean(x * x, axis=-1, keepdims=True) + NORM_EPS) * w


def _sigmoid(x):
    return 1.0 / (1.0 + jnp.exp(-x))


def _silu(x):
    return x * _sigmoid(x)


def _dot(a, b):
    return jnp.dot(a, b, preferred_element_type=F32)


def _pack_bf16_pairs(x):
    n = x.shape[1] // 2
    hi = lax.bitcast_convert_type(x[:, :n].astype(BF16).astype(F32), jnp.uint32)
    lo = lax.bitcast_convert_type(x[:, n:].astype(BF16).astype(F32), jnp.uint32)
    return hi | (lo >> 16)


def _unpack_bf16_pairs(p):
    hi = lax.bitcast_convert_type(p & jnp.uint32(0xFFFF0000), F32)
    lo = lax.bitcast_convert_type(p << 16, F32)
    return jnp.concatenate([hi, lo], axis=1)


def _store_row_groups(ref, packed):
    m, n = packed.shape
    g = n // LANES
    for c in range(g):
        ref[pl.ds(c, m, stride=g), :] = packed[:, c * LANES:(c + 1) * LANES]


def _load_row_groups(ref, m, g, start=0):
    return jnp.concatenate([ref[pl.ds(start + c, m, stride=g), :] for c in range(g)], axis=1)


def _mod_kernel(c_ref, w_ref, b_ref, o_ref):
    c = c_ref[...]
    o_ref[0] = jnp.dot(_silu(c), w_ref[0], preferred_element_type=F32,
                       precision=lax.Precision.HIGHEST) + b_ref[0]


def _modulation(cvec, mod_w, mod_b):
    depth, d, n = mod_w.shape
    rows = cvec.shape[0]
    tn = 1536
    return pl.pallas_call(
        _mod_kernel,
        grid=(depth, n // tn),
        in_specs=[pl.BlockSpec((rows, d), lambda l, j: (0, 0)),
                  pl.BlockSpec((1, d, tn), lambda l, j: (l, 0, j)),
                  pl.BlockSpec((1, 1, tn), lambda l, j: (l, 0, j))],
        out_specs=pl.BlockSpec((1, rows, tn), lambda l, j: (l, 0, j)),
        out_shape=jax.ShapeDtypeStruct((depth, rows, n), F32),
        compiler_params=_cparams(("arbitrary", "arbitrary")),
        name="modulation",
    )(cvec, mod_w, mod_b.reshape(depth, 1, n))


def _rope(v, cos, sin):
    n = v.shape[1]
    lane = lax.broadcasted_iota(jnp.int32, v.shape, 1)
    up = pltpu.roll(v, 8, axis=1)
    dn = pltpu.roll(v, n - 8, axis=1)
    rot = jnp.where(lane % 16 < 8, -dn, up)
    return v * cos + rot * sin


def _pre_mla_kernel(lat_ref, ctx_ref, mod_ref, nw_ref, wqa_ref, qn_ref, wqb_ref, wkva_ref, kvn_ref,
                    wkvb_ref, cos_ref, sin_ref, qn_out, qp_out, k_out, v_out, *, n_lat_tiles):
    x = jnp.where(pl.program_id(0) < n_lat_tiles, lat_ref[...], ctx_ref[...])
    m = mod_ref[0]
    h = _rms(x, nw_ref[...]) * (1.0 + m[1:2]) + m[0:1]
    hb = h.astype(BF16)
    cos = cos_ref[...]
    sin = sin_ref[...]

    qa = _rms(_dot(hb, wqa_ref[...]), qn_ref[...])
    scale = (QK_NOPE_DIM + QK_ROPE_DIM) ** -0.5 * LOG2_E
    q = _dot(qa.astype(BF16), wqb_ref[...]) * scale
    d_nope = N_HEADS * QK_NOPE_DIM
    q_pe = _rope(q[:, d_nope:], cos, sin)

    kva = _dot(hb, wkva_ref[...])
    kpe4_t = _rope(kva[:, KV_LORA_RANK:], cos[:, :LANES], sin[:, :LANES]).T.astype(BF16)
    ckv = _rms(kva[:, :KV_LORA_RANK], kvn_ref[...])
    kv = _dot(ckv.astype(BF16), wkvb_ref[...])

    for j in range(N_PAIRS):
        sl = slice(j * LANES, (j + 1) * LANES)
        qn_out[j] = q[:, sl].astype(BF16)
        k_out[j, :LANES, :] = kv[:, sl].T.astype(BF16)
        k_out[j, LANES:, :] = kpe4_t
        v_out[j] = kv[:, d_nope + j * LANES:d_nope + (j + 1) * LANES].astype(BF16)
    for g in range(N_ROPE_GROUPS):
        qp_out[g] = q_pe[:, g * LANES:(g + 1) * LANES].astype(BF16)


def _pre_mla(lat, ctx, mod, nw, wqa, qn, wqb, wkva, kvn, wkvb, cos, sin, *, nb, tpb, cpb):
    d = lat.shape[1]
    t = lat.shape[0] + ctx.shape[0]
    tm = TOKEN_TILE
    nx = nb * tpb
    spb = tpb + cpb

    def mrow(i):
        return jnp.where(i < nx, i // tpb, nb)

    def kvblk(i):
        ic = i - nx
        return jnp.where(i < nx, (i // tpb) * spb + cpb + i % tpb, (ic // cpb) * spb + ic % cpb)

    def ropeblk(i):
        return jnp.where(i < nx, i % tpb, tpb)

    full = lambda a: pl.BlockSpec(a.shape, lambda i: (0,) * a.ndim)
    s_rows = nb * spb * tm
    return pl.pallas_call(
        functools.partial(_pre_mla_kernel, n_lat_tiles=nx),
        grid=(t // tm,),
        in_specs=[pl.BlockSpec((tm, d), lambda i: (jnp.minimum(i, nx - 1), 0)),
                  pl.BlockSpec((tm, d), lambda i: (jnp.maximum(i - nx, 0), 0)),
                  pl.BlockSpec((1, 6, d), lambda i: (mrow(i), 0, 0)),
                  full(nw), full(wqa), full(qn), full(wqb), full(wkva), full(kvn), full(wkvb),
                  pl.BlockSpec((tm, cos.shape[1]), lambda i: (ropeblk(i), 0)),
                  pl.BlockSpec((tm, sin.shape[1]), lambda i: (ropeblk(i), 0))],
        out_specs=[pl.BlockSpec((N_PAIRS, tm, LANES), lambda i: (0, i, 0)),
                   pl.BlockSpec((N_ROPE_GROUPS, tm, LANES), lambda i: (0, i, 0)),
                   pl.BlockSpec((N_PAIRS, 2 * LANES, tm), lambda i: (0, 0, kvblk(i))),
                   pl.BlockSpec((N_PAIRS, tm, LANES), lambda i: (0, kvblk(i), 0))],
        out_shape=[jax.ShapeDtypeStruct((N_PAIRS, t, LANES), BF16),
                   jax.ShapeDtypeStruct((N_ROPE_GROUPS, t, LANES), BF16),
                   jax.ShapeDtypeStruct((N_PAIRS, 2 * LANES, s_rows), BF16),
                   jax.ShapeDtypeStruct((N_PAIRS, s_rows, LANES), BF16)],
        compiler_params=_cparams(("arbitrary",)),
        name="pre_mla",
    )(lat, ctx, mod, nw, wqa, qn, wqb, wkva, kvn, wkvb, cos, sin)


def _attn_tile(qn_ref, qp_ref, k_ref, v_ref, o_ref, s_len):
    tq = qn_ref.shape[1]
    lane = lax.broadcasted_iota(jnp.int32, (tq, LANES), 1)

    def pair(j, carry):
        qn = qn_ref[j]
        qp = qp_ref[j // 2]
        k = k_ref[j, :, :s_len]
        v = v_ref[j, :s_len, :]
        outs = []
        for e in range(2):
            qn_m = jnp.where(lane // QK_NOPE_DIM == e, qn, jnp.zeros_like(qn))
            qp_m = jnp.where(lane // QK_ROPE_DIM == 2 * (j % 2) + e, qp, jnp.zeros_like(qp))
            lhs = jnp.concatenate([qn_m, qp_m], axis=1)
            s = _dot(lhs, k)
            p = jnp.exp2(s - jnp.max(s, axis=-1, keepdims=True))
            l = jnp.sum(p, axis=-1, keepdims=True)
            outs.append(_dot(p.astype(BF16), v) * (1.0 / l))
        o_ref[j] = jnp.where(lane < V_HEAD_DIM, outs[0], outs[1]).astype(BF16)
        return carry

    lax.fori_loop(0, N_PAIRS, pair, 0, unroll=8)


def _attn_kernel(qn_ref, qp_ref, k_ref, v_ref, o_ref, *, tpb, n_ctx):
    step = pl.program_id(1)

    @pl.when(step < tpb)
    def _():
        _attn_tile(qn_ref, qp_ref, k_ref, v_ref, o_ref, v_ref.shape[1])

    @pl.when(step >= tpb)
    def _():
        _attn_tile(qn_ref, qp_ref, k_ref, v_ref, o_ref, n_ctx)


def _attention(qn, qp, kcat, v, *, nb, tpb, cpb):
    t = qn.shape[1]
    tq = TOKEN_TILE
    nx = nb * tpb
    s_len = (tpb + cpb) * tq

    def qblk(b, i):
        return jnp.where(i < tpb, b * tpb + i, nx + b * cpb + (i - tpb))

    return pl.pallas_call(
        functools.partial(_attn_kernel, tpb=tpb, n_ctx=cpb * tq),
        grid=(nb, tpb + cpb),
        in_specs=[pl.BlockSpec((N_PAIRS, tq, LANES), lambda b, i: (0, qblk(b, i), 0)),
                  pl.BlockSpec((N_ROPE_GROUPS, tq, LANES), lambda b, i: (0, qblk(b, i), 0)),
                  pl.BlockSpec((N_PAIRS, 2 * LANES, s_len), lambda b, i: (0, 0, b)),
                  pl.BlockSpec((N_PAIRS, s_len, LANES), lambda b, i: (0, b, 0))],
        out_specs=pl.BlockSpec((N_PAIRS, tq, LANES), lambda b, i: (0, qblk(b, i), 0)),
        out_shape=jax.ShapeDtypeStruct((N_PAIRS, t, LANES), BF16),
        compiler_params=_cparams(("arbitrary", "arbitrary")),
        name="attention",
    )(qn, qp, kcat, v)


def _post_kernel(act_ref, wmix_ref, bmix_ref, xs_ref, xs2_ref, mod_ref, nfw_ref, rwt_ref, rb_ref,
                 sw1_ref, sw3_ref, sw2_ref,
                 h2_out, base_out, idx_out, wt_out, rank_out, cnt_out, cnt_scr,
                 *, pair_major, n_first):
    i = pl.program_id(0)
    tm = xs_ref.shape[0]
    xs_in = jnp.where(i < n_first, xs_ref[...], xs2_ref[...])

    @pl.when(i == 0)
    def _():
        cnt_scr[...] = jnp.zeros_like(cnt_scr)

    if pair_major:
        act = jnp.concatenate([act_ref[j] for j in range(N_PAIRS)], axis=1)
    else:
        act = act_ref[...]
    m = mod_ref[0]
    mix = _dot(act, wmix_ref[...]) + bmix_ref[...]
    xs = xs_in + m[2:3] * mix
    h2 = _rms(xs, nfw_ref[...]) * (1.0 + m[4:5]) + m[3:4]
    _store_row_groups(h2_out, _pack_bf16_pairs(h2))

    h2b = h2.astype(BF16)
    hid = _silu(_dot(h2b, sw1_ref[...])) * _dot(h2b, sw3_ref[...])
    base_out[...] = xs + m[5:6] * _dot(hid.astype(BF16), sw2_ref[...])

    logits = lax.dot_general(rwt_ref[...], h2, (((1,), (1,)), ((), ())),
                             preferred_element_type=F32, precision=lax.Precision.HIGHEST)
    scores = _sigmoid(logits)
    sel = scores + rb_ref[...]
    per_group = N_EXPERTS // N_GROUPS
    neg = jnp.float32(-jnp.inf)
    sub_iota = lax.broadcasted_iota(jnp.int32, (per_group, tm), 0)
    gs_rows = []
    for g in range(N_GROUPS):
        sg = sel[g * per_group:(g + 1) * per_group, :]
        m1 = jnp.max(sg, axis=0, keepdims=True)
        first = jnp.min(jnp.where(sg == m1, sub_iota, per_group), axis=0, keepdims=True)
        m2 = jnp.max(jnp.where(sub_iota == first, neg, sg), axis=0, keepdims=True)
        gs_rows.append(m1 + m2)
    gs = jnp.concatenate(gs_rows, axis=0)
    g_iota = lax.broadcasted_iota(jnp.int32, (N_GROUPS, tm), 0)
    gmask = jnp.zeros((N_GROUPS, tm), F32)
    work = gs
    for _ in range(TOPK_GROUPS):
        mx = jnp.max(work, axis=0, keepdims=True)
        gi = jnp.min(jnp.where(work == mx, g_iota, N_GROUPS), axis=0, keepdims=True)
        pick = g_iota == gi
        gmask = jnp.where(pick, 1.0, gmask)
        work = jnp.where(pick, neg, work)
    emask = jnp.concatenate(
        [jnp.broadcast_to(gmask[g:g + 1, :], (per_group, tm)) for g in range(N_GROUPS)], axis=0)
    masked = jnp.where(emask > 0.5, sel, neg)
    e_iota = lax.broadcasted_iota(jnp.int32, (N_EXPERTS, tm), 0)
    picks, idx_rows, w_rows = [], [], []
    for _ in range(TOP_K):
        mx = jnp.max(masked, axis=0, keepdims=True)
        ei = jnp.min(jnp.where(masked == mx, e_iota, N_EXPERTS), axis=0, keepdims=True)
        pick = e_iota == ei
        picks.append(pick)
        idx_rows.append(ei)
        w_rows.append(jnp.sum(jnp.where(pick, scores, 0.0), axis=0, keepdims=True))
        masked = jnp.where(pick, neg, masked)
    wsum = w_rows[0]
    for r in w_rows[1:]:
        wsum = wsum + r
    wnorm = ROUTED_SCALE / wsum

    member = jnp.zeros((N_EXPERTS, tm), F32)
    for pick in picks:
        member = jnp.where(pick, 1.0, member)
    before = (lax.broadcasted_iota(jnp.int32, (tm, tm), 0)
              < lax.broadcasted_iota(jnp.int32, (tm, tm), 1))
    prefix = _dot(member.astype(BF16), jnp.where(before, 1.0, 0.0).astype(BF16))
    cnt = cnt_scr[...]
    rank_full = cnt[:, 0:1] + prefix
    rank_rows = [jnp.sum(jnp.where(pick, rank_full, 0.0), axis=0, keepdims=True)
                 for pick in picks]
    cnt = cnt + jnp.sum(member, axis=1, keepdims=True)
    cnt_scr[...] = cnt
    cnt_out[...] = cnt

    pad = SUBLANES - TOP_K
    zi = jnp.zeros((pad, tm), jnp.int32)
    zf = jnp.zeros((pad, tm), F32)
    idx_out[...] = jnp.concatenate(idx_rows + [zi], axis=0)
    wt_out[...] = jnp.concatenate([r * wnorm for r in w_rows] + [zf], axis=0)
    rank_out[...] = jnp.concatenate([r.astype(jnp.int32) for r in rank_rows] + [zi], axis=0)


def _post(act, wmix, bmix, xs, xs2, mod, nfw, rwt, rb, sw1, sw3, sw2, *, tm, nt, mrow,
          pair_major):
    d = xs.shape[1]
    t = nt * tm
    g = d // 2 // LANES
    n_first = min(nt, xs.shape[0] // tm)
    full = lambda a: pl.BlockSpec(a.shape, lambda i: (0,) * a.ndim)
    if pair_major:
        act_spec = pl.BlockSpec((N_PAIRS, tm, LANES), lambda i: (0, i, 0))
    else:
        act_spec = pl.BlockSpec((tm, act.shape[1]), lambda i: (i, 0))
    row8 = pl.BlockSpec((SUBLANES, tm), lambda i: (0, i))
    return pl.pallas_call(
        functools.partial(_post_kernel, pair_major=pair_major, n_first=n_first),
        grid=(nt,),
        in_specs=[act_spec, full(wmix), full(bmix),
                  pl.BlockSpec((tm, d), lambda i: (jnp.minimum(i, n_first - 1), 0)),
                  pl.BlockSpec((tm, d), lambda i: (jnp.maximum(i - n_first, 0), 0)),
                  pl.BlockSpec((1, 6, d), lambda i: (mrow(i), 0, 0)),
                  full(nfw), full(rwt), full(rb), full(sw1), full(sw3), full(sw2)],
        out_specs=[pl.BlockSpec((tm * g, LANES), lambda i: (i, 0)),
                   pl.BlockSpec((tm, d), lambda i: (i, 0)),
                   row8, row8, row8,
                   pl.BlockSpec((N_EXPERTS, LANES), lambda i: (0, 0))],
        out_shape=[jax.ShapeDtypeStruct((t * g, LANES), jnp.uint32),
                   jax.ShapeDtypeStruct((t, d), F32),
                   jax.ShapeDtypeStruct((SUBLANES, t), jnp.int32),
                   jax.ShapeDtypeStruct((SUBLANES, t), F32),
                   jax.ShapeDtypeStruct((SUBLANES, t), jnp.int32),
                   jax.ShapeDtypeStruct((N_EXPERTS, LANES), F32)],
        scratch_shapes=[pltpu.VMEM((N_EXPERTS, LANES), F32)],
        compiler_params=_cparams(("arbitrary",)),
        name="post_mixer",
    )(act, wmix, bmix, xs, xs2, mod, nfw, rwt, rb, sw1, sw3, sw2)


def _dispatch_kernel(dest_ref, fill_ref, h2_ref, xg_out, zbuf, sem, zsem, *, t_total, tile, g):
    step = pl.program_id(0)
    base = step * tile
    rows = zbuf.shape[0]
    n_blocks = xg_out.shape[0] // rows

    @pl.when(step == 0)
    def _():
        zbuf[...] = jnp.zeros_like(zbuf)

        def fill_copy(n):
            return pltpu.make_async_copy(
                zbuf, xg_out.at[pl.ds(pl.multiple_of(n * rows, rows), rows)], zsem)

        def start(n, carry):
            @pl.when(fill_ref[n] > 0)
            def _():
                fill_copy(n).start()
            return carry

        def wait(n, carry):
            @pl.when(fill_ref[n] > 0)
            def _():
                fill_copy(n).wait()
            return carry

        lax.fori_loop(0, n_blocks, start, 0)
        lax.fori_loop(0, n_blocks, wait, 0)

    def row_copy(r, slot):
        return pltpu.make_async_copy(h2_ref.at[pl.ds(pl.multiple_of(r * g, g), g)],
                                     xg_out.at[pl.ds(pl.multiple_of(slot * g, g), g)], sem)

    def issue(r, carry):
        for k in range(TOP_K):
            row_copy(r, dest_ref[k * t_total + base + r]).start(priority=k % 2)
        return carry

    lax.fori_loop(0, tile, issue, 0, unroll=ISSUE_UNROLL)

    def drain(r, carry):
        for k in range(TOP_K):
            row_copy(0, 0).wait()
        return carry

    lax.fori_loop(0, tile, drain, 0, unroll=ISSUE_UNROLL)


def _dispatch(dest, fill, h2, n_slots, g):
    t = h2.shape[0] // g
    tile = DISPATCH_TILE
    return pl.pallas_call(
        functools.partial(_dispatch_kernel, t_total=t, tile=tile, g=g),
        grid_spec=pltpu.PrefetchScalarGridSpec(
            num_scalar_prefetch=2,
            grid=(t // tile,),
            in_specs=[pl.BlockSpec((tile * g, LANES), lambda i, dr, fr: (i, 0))],
            out_specs=pl.BlockSpec(memory_space=pl.ANY),
            scratch_shapes=[pltpu.VMEM((EXPERT_ROWS * g, LANES), h2.dtype),
                            pltpu.SemaphoreType.DMA(()), pltpu.SemaphoreType.DMA(())]),
        out_shape=jax.ShapeDtypeStruct((n_slots * g, LANES), h2.dtype),
        compiler_params=_cparams(("arbitrary",)),
        name="moe_dispatch",
    )(dest, fill, h2)


def _expert_kernel(be_ref, nu_ref, first_ref, slot_ref, nxt_ref, x_ref, w1_hbm, w3_hbm, w2_hbm,
                   y_ref, wf1, wf3, wf2, w1b, w3b, w2b, sems, *, g, layer):
    n = pl.program_id(0)
    used = n < nu_ref[0]
    rows = x_ref.shape[0] // g

    def weight_copies(e, s):
        return (pltpu.make_async_copy(w1_hbm.at[layer, e], wf1.at[s], sems.at[s]),
                pltpu.make_async_copy(w3_hbm.at[layer, e], wf3.at[s], sems.at[s]),
                pltpu.make_async_copy(w2_hbm.at[layer, e], wf2.at[s], sems.at[s]))

    @pl.when(n == 0)
    def _():
        for c in weight_copies(be_ref[0], 0):
            c.start()

    @pl.when(first_ref[n] > 0)
    def _():
        s = slot_ref[n]
        for c in weight_copies(be_ref[n], s):
            c.wait()
        w1b[...] = wf1[s].astype(BF16)
        w3b[...] = wf3[s].astype(BF16)
        w2b[...] = wf2[s].astype(BF16)

        @pl.when(nxt_ref[n] >= 0)
        def _():
            for c in weight_copies(nxt_ref[n], 1 - s):
                c.start()

    @pl.when(used)
    def _():
        x = _unpack_bf16_pairs(_load_row_groups(x_ref, rows, g)).astype(BF16)
        hid = _silu(_dot(x, w1b[...])) * _dot(x, w3b[...])
        _store_row_groups(y_ref, _pack_bf16_pairs(_dot(hid.astype(BF16), w2b[...])))

    @pl.when(jnp.logical_not(used))
    def _():
        y_ref[...] = jnp.zeros_like(y_ref)


def _experts(blk_e, n_used, xg, w1, w3, w2, g, layer):
    rows = EXPERT_ROWS * g
    nblk = xg.shape[0] // rows
    _, _, d, e_dim = w1.shape

    pos = jnp.arange(nblk, dtype=jnp.int32)
    first = jnp.logical_or(pos == 0, blk_e != jnp.roll(blk_e, 1))
    slot = (jnp.cumsum(first.astype(jnp.int32)) - 1) % 2
    next_first = lax.cummin(jnp.where(first, pos, nblk), reverse=True)
    next_first = jnp.concatenate([next_first[1:], jnp.full((1,), nblk, jnp.int32)])
    nxt = jnp.where(next_first < nblk, jnp.take(blk_e, jnp.minimum(next_first, nblk - 1)), -1)

    def xmap(n, be, nu, fr, sr, nr):
        return (jnp.minimum(n, nu[0] - 1), 0)

    return pl.pallas_call(
        functools.partial(_expert_kernel, g=g, layer=layer),
        grid_spec=pltpu.PrefetchScalarGridSpec(
            num_scalar_prefetch=5,
            grid=(nblk,),
            in_specs=[pl.BlockSpec((rows, LANES), xmap),
                      pl.BlockSpec(memory_space=pl.ANY),
                      pl.BlockSpec(memory_space=pl.ANY),
                      pl.BlockSpec(memory_space=pl.ANY)],
            out_specs=pl.BlockSpec((rows, LANES), lambda n, be, nu, fr, sr, nr: (n, 0)),
            scratch_shapes=[pltpu.VMEM((2, d, e_dim), F32), pltpu.VMEM((2, d, e_dim), F32),
                            pltpu.VMEM((2, e_dim, d), F32),
                            pltpu.VMEM((d, e_dim), BF16), pltpu.VMEM((d, e_dim), BF16),
                            pltpu.VMEM((e_dim, d), BF16),
                            pltpu.SemaphoreType.DMA((2,))]),
        out_shape=jax.ShapeDtypeStruct(xg.shape, jnp.uint32),
        compiler_params=_cparams(("arbitrary",)),
        name="moe_experts",
    )(blk_e, n_used, first.astype(jnp.int32), slot.astype(jnp.int32), nxt.astype(jnp.int32),
      xg, w1, w3, w2)


def _combine_kernel(dest_ref, y_hbm, base_ref, wt_ref, mod_ref, fnw_ref, o_ref, buf, sems,
                    *, t_total, n_tiles, final_norm, g):
    tm = base_ref.shape[0]
    step = pl.program_id(0)
    cur = step % 2
    half = TOP_K * tm * g
    chunk = SUBLANES
    gate = mod_ref[0][5:6]
    fnw = fnw_ref[...]

    def row_copy(slot, which, k, r):
        dst = pl.multiple_of(which * half + (k * tm + r) * g, g)
        return pltpu.make_async_copy(y_hbm.at[pl.ds(pl.multiple_of(slot * g, g), g)],
                                     buf.at[pl.ds(dst, g)], sems.at[which])

    def issue_chunk(tile, which, r0):
        for dr in range(chunk):
            for k in range(TOP_K):
                slot = dest_ref[k * t_total + tile * tm + r0 + dr]
                row_copy(slot, which, k, r0 + dr).start(priority=k % 2)

    def sum_chunk(r0):
        wt = wt_ref[pl.ds(r0, chunk), :]
        acc = None
        for k in range(TOP_K):
            start = cur * half + (k * tm + r0) * g
            packed = _load_row_groups(buf, chunk, g, start=start)
            term = wt[:, k:k + 1] * _unpack_bf16_pairs(packed)
            acc = term if acc is None else acc + term
        out = base_ref[pl.ds(r0, chunk), :] + gate * acc
        if final_norm:
            out = _rms(out, fnw)
        o_ref[pl.ds(r0, chunk), :] = out

    @pl.when(step == 0)
    def _():
        def first(c, carry):
            issue_chunk(0, 0, pl.multiple_of(c * chunk, chunk))
            return carry
        lax.fori_loop(0, tm // chunk, first, 0)

    def drain(r, carry):
        for k in range(TOP_K):
            row_copy(0, cur, k, 0).wait()
        return carry

    lax.fori_loop(0, tm, drain, 0, unroll=ISSUE_UNROLL)

    @pl.when(step + 1 < n_tiles)
    def _():
        def body(c, carry):
            r0 = pl.multiple_of(c * chunk, chunk)
            sum_chunk(r0)
            issue_chunk(step + 1, 1 - cur, r0)
            return carry
        lax.fori_loop(0, tm // chunk, body, 0, unroll=2)

    @pl.when(step + 1 == n_tiles)
    def _():
        def body(c, carry):
            sum_chunk(pl.multiple_of(c * chunk, chunk))
            return carry
        lax.fori_loop(0, tm // chunk, body, 0)


def _combine(dest, yg, base, wt_t, mod, fnw, *, nt, mrow, final_norm, g):
    d = base.shape[1]
    tm = COMBINE_TILE
    t = nt * tm
    return pl.pallas_call(
        functools.partial(_combine_kernel, t_total=base.shape[0], n_tiles=nt,
                          final_norm=final_norm, g=g),
        grid_spec=pltpu.PrefetchScalarGridSpec(
            num_scalar_prefetch=1,
            grid=(nt,),
            in_specs=[pl.BlockSpec(memory_space=pl.ANY),
                      pl.BlockSpec((tm, d), lambda i, dr: (i, 0)),
                      pl.BlockSpec((tm, SUBLANES), lambda i, dr: (i, 0)),
                      pl.BlockSpec((1, 6, d), lambda i, dr: (mrow(i), 0, 0)),
                      pl.BlockSpec((1, d), lambda i, dr: (0, 0))],
            out_specs=pl.BlockSpec((tm, d), lambda i, dr: (i, 0)),
            scratch_shapes=[pltpu.VMEM((2 * TOP_K * tm * g, LANES), jnp.uint32),
                            pltpu.SemaphoreType.DMA((2,))]),
        out_shape=jax.ShapeDtypeStruct((t, d), F32),
        compiler_params=_cparams(("arbitrary",)),
        name="moe_combine",
    )(dest, yg, base, wt_t, mod, fnw)


def _moe(h2, base, idx8, wt8, rank8, cnt, mod, fnw, w1, w3, w2, *, layer, nt_out, mrow,
         final_norm):
    g = base.shape[1] // 2 // LANES
    t = h2.shape[0] // g
    rows = EXPERT_ROWS
    counts = cnt[:, 0].astype(jnp.int32)
    padded = (counts + rows - 1) // rows * rows
    pad_end = jnp.cumsum(padded)
    pad_start = pad_end - padded
    nblk = (t * TOP_K + rows - 1) // rows + N_EXPERTS
    e_ids = jnp.arange(N_EXPERTS, dtype=jnp.int32)
    idx = idx8[:TOP_K]
    start_of = jnp.sum(jnp.where(idx[:, :, None] == e_ids, pad_start, 0), axis=-1)
    dest = (start_of + rank8[:TOP_K]).reshape(-1)
    n_used = (pad_end[-1] // rows).astype(jnp.int32)
    blk_start = jnp.arange(nblk, dtype=jnp.int32) * rows
    e_of_blk = jnp.sum((pad_end[None, :] <= blk_start[:, None]).astype(jnp.int32), axis=1)
    valid_end = jnp.sum(jnp.where(e_of_blk[:, None] == e_ids, pad_start + counts, 0), axis=1)
    fill = (blk_start + rows > valid_end).astype(jnp.int32)
    e_last = jnp.max(jnp.where(counts > 0, e_ids, 0))
    blk_e = jnp.minimum(e_of_blk, e_last).astype(jnp.int32)
    xg = _dispatch(dest, fill, h2, nblk * rows, g)
    yg = _experts(blk_e, n_used.reshape(1), xg, w1, w3, w2, g, layer)
    tile_ratio = TOKEN_TILE // COMBINE_TILE
    return _combine(dest, yg, base, wt8.T, mod, fnw, nt=nt_out * tile_ratio,
                    mrow=lambda i: mrow(i // tile_ratio), final_norm=final_norm, g=g)


def _pre_conv_kernel(xs_ref, mod_ref, nw_ref, w_ref, b_ref, o_ref):
    m = mod_ref[0]
    h = _rms(xs_ref[...], nw_ref[...]) * (1.0 + m[1:2]) + m[0:1]
    u = _dot(h.astype(BF16), w_ref[...]) + b_ref[...]
    d = o_ref.shape[1]
    o_ref[...] = u[:, :d] * _sigmoid(u[:, d:])


def _pre_conv(xs, mod, nw, w, b, *, nt, mrow):
    d = xs.shape[1]
    tm = TOKEN_TILE
    full = lambda a: pl.BlockSpec(a.shape, lambda i: (0,) * a.ndim)
    return pl.pallas_call(
        _pre_conv_kernel,
        grid=(nt,),
        in_specs=[pl.BlockSpec((tm, d), lambda i: (i, 0)),
                  pl.BlockSpec((1, 6, d), lambda i: (mrow(i), 0, 0)),
                  full(nw), full(w), full(b)],
        out_specs=pl.BlockSpec((tm, d), lambda i: (i, 0)),
        out_shape=jax.ShapeDtypeStruct((nt * tm, d), F32),
        compiler_params=_cparams(("arbitrary",)),
        name="pre_conv",
    )(xs, mod, nw, w, b)


def _conv_kernel(prev_ref, cur_ref, next_ref, dw_ref, db_ref, lw_ref, lb_ref, o_ref, win, conv,
                 *, tpb):
    tm, d = cur_ref.shape
    r = pl.program_id(0) % tpb
    halo = CONV_HALO
    pad = CONV_WIDTH // 2
    win[0:halo, :] = jnp.where(r > 0, prev_ref[...], 0.0)
    win[halo:halo + tm, :] = cur_ref[...]
    win[halo + tm:halo + tm + halo, :] = jnp.where(r < tpb - 1, next_ref[...], 0.0)

    rows = 64
    for c in range(d // LANES):
        cs = slice(c * LANES, (c + 1) * LANES)
        taps = dw_ref[:, cs]
        bias = db_ref[:, cs]

        def chunk(q, carry):
            r0 = pl.multiple_of(q * rows, rows)
            slab = win[pl.ds(r0, rows + 2 * halo), cs]
            acc = jnp.zeros((rows, LANES), F32) + bias
            for b in range(SUBLANES):
                part = None
                for j in range(CONV_WIDTH):
                    off = halo - pad + j
                    if off % SUBLANES == b:
                        a = off - b
                        term = taps[j:j + 1, :] * slab[a:a + rows + SUBLANES, :]
                        part = term if part is None else part + term
                acc = acc + part[b:b + rows, :]
            conv[pl.ds(r0, rows), cs] = acc
            return carry

        lax.fori_loop(0, tm // rows, chunk, 0)

    u = conv[...]
    mu = jnp.mean(u, axis=-1, keepdims=True)
    var = jnp.mean(jnp.square(u - mu), axis=-1, keepdims=True)
    y = (u - mu) * lax.rsqrt(var + NORM_EPS) * lw_ref[...] + lb_ref[...]
    o_ref[...] = _silu(y).astype(BF16)


def _conv(glu, dw, db, lw, lb, *, nt, tpb):
    d = glu.shape[1]
    tm = TOKEN_TILE
    hb = tm // CONV_HALO
    n_halo_blocks = glu.shape[0] // CONV_HALO
    full = lambda a: pl.BlockSpec(a.shape, lambda i: (0,) * a.ndim)
    return pl.pallas_call(
        functools.partial(_conv_kernel, tpb=tpb),
        grid=(nt,),
        in_specs=[pl.BlockSpec((CONV_HALO, d), lambda i: (jnp.maximum(i * hb - 1, 0), 0)),
                  pl.BlockSpec((tm, d), lambda i: (i, 0)),
                  pl.BlockSpec((CONV_HALO, d),
                               lambda i: (jnp.minimum((i + 1) * hb, n_halo_blocks - 1), 0)),
                  full(dw), full(db), full(lw), full(lb)],
        out_specs=pl.BlockSpec((tm, d), lambda i: (i, 0)),
        out_shape=jax.ShapeDtypeStruct((nt * tm, d), BF16),
        scratch_shapes=[pltpu.VMEM((tm + 2 * CONV_HALO, d), F32), pltpu.VMEM((tm, d), F32)],
        compiler_params=_cparams(("arbitrary",)),
        name="dw_conv",
    )(glu, glu, glu, dw, db, lw, lb)


def _rope_tables(n_tokens, extra_rows):
    rows = n_tokens // GRID_W
    row = jnp.repeat(jnp.arange(rows, dtype=F32), GRID_W)
    col = jnp.tile(jnp.arange(GRID_W, dtype=F32), rows)
    axis_dim = QK_ROPE_DIM // 2
    inv_freq = ROPE_THETA ** (-jnp.arange(0, axis_dim, 2, dtype=F32) / axis_dim)
    ang_r = row[:, None] * inv_freq
    ang_c = col[:, None] * inv_freq
    ang = jnp.concatenate([ang_r, ang_r, ang_c, ang_c], axis=-1)
    cos = jnp.concatenate([jnp.cos(ang), jnp.ones((extra_rows, QK_ROPE_DIM), F32)], axis=0)
    sin = jnp.concatenate([jnp.sin(ang), jnp.zeros((extra_rows, QK_ROPE_DIM), F32)], axis=0)
    return jnp.tile(cos, (1, N_HEADS)), jnp.tile(sin, (1, N_HEADS))


def kernel(x, c, ctx, c_ctx, mod_w, mod_b, norm_mix_w, norm_ffn_w, mla_wq_a, mla_q_norm, mla_wq_b, mla_wkv_a, mla_kv_norm, mla_wkv_b, mla_wo, conv_pw1_w, conv_pw1_b, conv_dw_w, conv_dw_b, conv_norm_w, conv_norm_b, conv_pw2_w, conv_pw2_b, router_w, router_bias, exp_w1, exp_w3, exp_w2, shared_w1, shared_w3, shared_w2, final_norm_w):
    nb, seq, d = x.shape
    n_ctx = ctx.shape[1]
    tm = TOKEN_TILE
    tpb, cpb = seq // tm, n_ctx // tm
    nx, nc = nb * tpb, nb * cpb
    row = lambda a: a.reshape(1, -1)

    pad_rows = (-(nb + 1)) % SUBLANES
    cvec = jnp.concatenate([c, c_ctx[None, :], jnp.zeros((pad_rows, d), F32)], axis=0)
    mod = _modulation(cvec, mod_w, mod_b).reshape(mod_w.shape[0], cvec.shape[0], 6, d)

    def mrow_all(i):
        return jnp.where(i < nx, i // tpb, nb)

    lat2d, ctx2d = x.reshape(nb * seq, d), ctx.reshape(nb * n_ctx, d)
    wqb =mla_wq_b[0].reshape(Q_LORA_RANK, N_HEADS, QK_NOPE_DIM + QK_ROPE_DIM)
    wqb = jnp.concatenate([wqb[:, :, :QK_NOPE_DIM].reshape(Q_LORA_RANK, -1),
                           wqb[:, :, QK_NOPE_DIM:].reshape(Q_LORA_RANK, -1)], axis=1)
    wkva = mla_wkv_a[0]
    wkva = jnp.concatenate([wkva[:, :KV_LORA_RANK]]
                           + [wkva[:, KV_LORA_RANK:]] * (LANES // QK_ROPE_DIM), axis=1)
    wkvb = mla_wkv_b[0].reshape(KV_LORA_RANK, N_HEADS, QK_NOPE_DIM + V_HEAD_DIM)
    wkvb = jnp.concatenate([wkvb[:, :, :QK_NOPE_DIM].reshape(KV_LORA_RANK, -1),
                            wkvb[:, :, QK_NOPE_DIM:].reshape(KV_LORA_RANK, -1)], axis=1)
    cos, sin = _rope_tables(seq, tm)
    qn, qp, kcat, v = _pre_mla(
        lat2d, ctx2d, mod[0], row(norm_mix_w[0]), mla_wq_a[0].astype(BF16), row(mla_q_norm[0]),
        wqb.astype(BF16), wkva.astype(BF16), row(mla_kv_norm[0]), wkvb.astype(BF16), cos, sin,
        nb=nb, tpb=tpb, cpb=cpb)
    o = _attention(qn, qp, kcat, v, nb=nb, tpb=tpb, cpb=cpb)

    moe_w = (exp_w1, exp_w3, exp_w2)
    post_w = lambda i: (row(norm_ffn_w[i]), router_w[i].T, router_bias[i].reshape(-1, 1),
                        shared_w1[i].astype(BF16), shared_w3[i].astype(BF16),
                        shared_w2[i].astype(BF16))
    ptile = POST_TILE if seq % POST_TILE == 0 and (nb * n_ctx) % POST_TILE == 0 else tm
    ppb = seq // ptile
    h2, base, idx8, wt8, rank8, cnt = _post(
        o, mla_wo[0].astype(BF16), jnp.zeros((1, d), F32), lat2d, ctx2d, mod[0], *post_w(0),
        tm=ptile, nt=(nb * (seq + n_ctx)) // ptile,
        mrow=lambda i: jnp.where(i < nb * ppb, i // ppb, nb), pair_major=True)
    xs = _moe(h2, base, idx8, wt8, rank8, cnt, mod[0], row(final_norm_w), *moe_w,
              layer=0, nt_out=nx + nc, mrow=mrow_all, final_norm=False)

    mrow_x = lambda i: i // tpb
    glu = _pre_conv(xs, mod[1], row(norm_mix_w[1]), conv_pw1_w[0].astype(BF16),
                    row(conv_pw1_b[0]), nt=nx, mrow=mrow_x)
    taps = jnp.concatenate([conv_dw_w[0], jnp.zeros((1, d), F32)], axis=0)
    act = _conv(glu, taps, row(conv_dw_b[0]), row(conv_norm_w[0]), row(conv_norm_b[0]),
                nt=nx, tpb=tpb)
    h2, base, idx8, wt8, rank8, cnt = _post(
        act, conv_pw2_w[0].astype(BF16), row(conv_pw2_b[0]), xs, xs, mod[1],
        *post_w(1), tm=ptile, nt=nb * ppb, mrow=lambda i: i // ppb, pair_major=False)
    out = _moe(h2, base, idx8, wt8, rank8, cnt, mod[1], row(final_norm_w), *moe_w,
               layer=1, nt_out=nx, mrow=mrow_x, final_norm=True)
    return out.reshape(nb, seq, d)
```

```python
import functools

import jax
import jax.numpy as jnp
from jax import lax
from jax.experimental import pallas as pl
from jax.experimental.pallas import tpu as pltpu

N_HEADS = 16
QK_NOPE_DIM = 64
QK_ROPE_DIM = 32
V_HEAD_DIM = 64
Q_LORA_RANK = 512
KV_LORA_RANK = 256
ROPE_THETA = 10000.0
GRID_W = 64
CONV_WIDTH = 31
N_EXPERTS = 64
TOP_K = 6
N_GROUPS = 8
TOPK_GROUPS = 4
ROUTED_SCALE = 2.5
NORM_EPS = 1e-6
LOG2_E = 1.4426950408889634

LANES = 128
SUBLANES = 8
TOKEN_TILE = 256
POST_TILE = 512
EXPERT_ROWS = 512
DISPATCH_TILE = 256
COMBINE_TILE = 256
ISSUE_UNROLL = 4
CONV_HALO = 16
N_PAIRS = N_HEADS // 2
N_ROPE_GROUPS = N_HEADS // 4
VMEM_LIMIT = 56 * 1024 * 1024

F32 = jnp.float32
BF16 = jnp.bfloat16


def _cparams(sem):
    return pltpu.CompilerParams(dimension_semantics=sem, vmem_limit_bytes=VMEM_LIMIT)


def _rms(x, w):
    return x * lax.rsqrt(jnp.mean(x * x, axis=-1, keepdims=True) + NORM_EPS) * w


def _sigmoid(x):
    return 1.0 / (1.0 + jnp.exp(-x))


def _silu(x):
    return x * _sigmoid(x)


def _dot(a, b):
    return jnp.dot(a, b, preferred_element_type=F32)


def _pack_bf16_pairs(x):
    n = x.shape[1] // 2
    hi = lax.bitcast_convert_type(x[:, :n].astype(BF16).astype(F32), jnp.uint32)
    lo = lax.bitcast_convert_type(x[:, n:].astype(BF16).astype(F32), jnp.uint32)
    return hi | (lo >> 16)


def _unpack_bf16_pairs(p):
    hi = lax.bitcast_convert_type(p & jnp.uint32(0xFFFF0000), F32)
    lo = lax.bitcast_convert_type(p << 16, F32)
    return jnp.concatenate([hi, lo], axis=1)


def _store_row_groups(ref, packed):
    m, n = packed.shape
    g = n // LANES
    for c in range(g):
        ref[pl.ds(c, m, stride=g), :] = packed[:, c * LANES:(c + 1) * LANES]


def _load_row_groups(ref, m, g, start=0):
    return jnp.concatenate([ref[pl.ds(start + c, m, stride=g), :] for c in range(g)], axis=1)


def _mod_kernel(c_ref, w_ref, b_ref, o_ref):
    c = c_ref[...]
    o_ref[0] = jnp.dot(_silu(c), w_ref[0], preferred_element_type=F32,
                       precision=lax.Precision.HIGHEST) + b_ref[0]


def _modulation(cvec, mod_w, mod_b):
    depth, d, n = mod_w.shape
    rows = cvec.shape[0]
    tn = 1536
    return pl.pallas_call(
        _mod_kernel,
        grid=(depth, n // tn),
        in_specs=[pl.BlockSpec((rows, d), lambda l, j: (0, 0)),
                  pl.BlockSpec((1, d, tn), lambda l, j: (l, 0, j)),
                  pl.BlockSpec((1, 1, tn), lambda l, j: (l, 0, j))],
        out_specs=pl.BlockSpec((1, rows, tn), lambda l, j: (l, 0, j)),
        out_shape=jax.ShapeDtypeStruct((depth, rows, n), F32),
        compiler_params=_cparams(("arbitrary", "arbitrary")),
        name="modulation",
    )(cvec, mod_w, mod_b.reshape(depth, 1, n))


def _rope(v, cos, sin):
    n = v.shape[1]
    lane = lax.broadcasted_iota(jnp.int32, v.shape, 1)
    up = pltpu.roll(v, 8, axis=1)
    dn = pltpu.roll(v, n - 8, axis=1)
    rot = jnp.where(lane % 16 < 8, -dn, up)
    return v * cos + rot * sin


def _pre_mla_kernel(lat_ref, ctx_ref, mod_ref, nw_ref, wqa_ref, qn_ref, wqb_ref, wkva_ref, kvn_ref,
                    wkvb_ref, cos_ref, sin_ref, qn_out, qp_out, k_out, v_out, *, n_lat_tiles):
    x = jnp.where(pl.program_id(0) < n_lat_tiles, lat_ref[...], ctx_ref[...])
    m = mod_ref[0]
    h = _rms(x, nw_ref[...]) * (1.0 + m[1:2]) + m[0:1]
    hb = h.astype(BF16)
    cos = cos_ref[...]
    sin = sin_ref[...]

    qa = _rms(_dot(hb, wqa_ref[...]), qn_ref[...])
    scale = (QK_NOPE_DIM + QK_ROPE_DIM) ** -0.5 * LOG2_E
    q = _dot(qa.astype(BF16), wqb_ref[...]) * scale
    d_nope = N_HEADS * QK_NOPE_DIM
    q_pe = _rope(q[:, d_nope:], cos, sin)

    kva = _dot(hb, wkva_ref[...])
    kpe4_t = _rope(kva[:, KV_LORA_RANK:], cos[:, :LANES], sin[:, :LANES]).T.astype(BF16)
    ckv = _rms(kva[:, :KV_LORA_RANK], kvn_ref[...])
    kv = _dot(ckv.astype(BF16), wkvb_ref[...])

    for j in range(N_PAIRS):
        sl = slice(j * LANES, (j + 1) * LANES)
        qn_out[j] = q[:, sl].astype(BF16)
        k_out[j, :LANES, :] = kv[:, sl].T.astype(BF16)
        k_out[j, LANES:, :] = kpe4_t
        v_out[j] = kv[:, d_nope + j * LANES:d_nope + (j + 1) * LANES].astype(BF16)
    for g in range(N_ROPE_GROUPS):
        qp_out[g] = q_pe[:, g * LANES:(g + 1) * LANES].astype(BF16)


def _pre_mla(lat, ctx, mod, nw, wqa, qn, wqb, wkva, kvn, wkvb, cos, sin, *, nb, tpb, cpb):
    d = lat.shape[1]
    t = lat.shape[0] + ctx.shape[0]
    tm = TOKEN_TILE
    nx = nb * tpb
    spb = tpb + cpb

    def mrow(i):
        return jnp.where(i < nx, i // tpb, nb)

    def kvblk(i):
        ic = i - nx
        return jnp.where(i < nx, (i // tpb) * spb + cpb + i % tpb, (ic // cpb) * spb + ic % cpb)

    def ropeblk(i):
        return jnp.where(i < nx, i % tpb, tpb)

    full = lambda a: pl.BlockSpec(a.shape, lambda i: (0,) * a.ndim)
    s_rows = nb * spb * tm
    return pl.pallas_call(
        functools.partial(_pre_mla_kernel, n_lat_tiles=nx),
        grid=(t // tm,),
        in_specs=[pl.BlockSpec((tm, d), lambda i: (jnp.minimum(i, nx - 1), 0)),
                  pl.BlockSpec((tm, d), lambda i: (jnp.maximum(i - nx, 0), 0)),
                  pl.BlockSpec((1, 6, d), lambda i: (mrow(i), 0, 0)),
                  full(nw), full(wqa), full(qn), full(wqb), full(wkva), full(kvn), full(wkvb),
                  pl.BlockSpec((tm, cos.shape[1]), lambda i: (ropeblk(i), 0)),
                  pl.BlockSpec((tm, sin.shape[1]), lambda i: (ropeblk(i), 0))],
        out_specs=[pl.BlockSpec((N_PAIRS, tm, LANES), lambda i: (0, i, 0)),
                   pl.BlockSpec((N_ROPE_GROUPS, tm, LANES), lambda i: (0, i, 0)),
                   pl.BlockSpec((N_PAIRS, 2 * LANES, tm), lambda i: (0, 0, kvblk(i))),
                   pl.BlockSpec((N_PAIRS, tm, LANES), lambda i: (0, kvblk(i), 0))],
        out_shape=[jax.ShapeDtypeStruct((N_PAIRS, t, LANES), BF16),
                   jax.ShapeDtypeStruct((N_ROPE_GROUPS, t, LANES), BF16),
                   jax.ShapeDtypeStruct((N_PAIRS, 2 * LANES, s_rows), BF16),
                   jax.ShapeDtypeStruct((N_PAIRS, s_rows, LANES), BF16)],
        compiler_params=_cparams(("arbitrary",)),
        name="pre_mla",
    )(lat, ctx, mod, nw, wqa, qn, wqb, wkva, kvn, wkvb, cos, sin)


def _attn_tile(qn_ref, qp_ref, k_ref, v_ref, o_ref, s_len):
    tq = qn_ref.shape[1]
    lane = lax.broadcasted_iota(jnp.int32, (tq, LANES), 1)

    def pair(j, carry):
        qn = qn_ref[j]
        qp = qp_ref[j // 2]
        k = k_ref[j, :, :s_len]
        v = v_ref[j, :s_len, :]
        outs = []
        for e in range(2):
            qn_m = jnp.where(lane // QK_NOPE_DIM == e, qn, jnp.zeros_like(qn))
            qp_m = jnp.where(lane // QK_ROPE_DIM == 2 * (j % 2) + e, qp, jnp.zeros_like(qp))
            lhs = jnp.concatenate([qn_m, qp_m], axis=1)
            s = _dot(lhs, k)
            p = jnp.exp2(s - jnp.max(s, axis=-1, keepdims=True))
            l = jnp.sum(p, axis=-1, keepdims=True)
            outs.append(_dot(p.astype(BF16), v) * (1.0 / l))
        o_ref[j] = jnp.where(lane < V_HEAD_DIM, outs[0], outs[1]).astype(BF16)
        return carry

    lax.fori_loop(0, N_PAIRS, pair, 0, unroll=8)


def _attn_kernel(qn_ref, qp_ref, k_ref, v_ref, o_ref, *, tpb, n_ctx):
    step = pl.program_id(1)

    @pl.when(step < tpb)
    def _():
        _attn_tile(qn_ref, qp_ref, k_ref, v_ref, o_ref, v_ref.shape[1])

    @pl.when(step >= tpb)
    def _():
        _attn_tile(qn_ref, qp_ref, k_ref, v_ref, o_ref, n_ctx)


def _attention(qn, qp, kcat, v, *, nb, tpb, cpb):
    t = qn.shape[1]
    tq = TOKEN_TILE
    nx = nb * tpb
    s_len = (tpb + cpb) * tq

    def qblk(b, i):
        return jnp.where(i < tpb, b * tpb + i, nx + b * cpb + (i - tpb))

    return pl.pallas_call(
        functools.partial(_attn_kernel, tpb=tpb, n_ctx=cpb * tq),
        grid=(nb, tpb + cpb),
        in_specs=[pl.BlockSpec((N_PAIRS, tq, LANES), lambda b, i: (0, qblk(b, i), 0)),
                  pl.BlockSpec((N_ROPE_GROUPS, tq, LANES), lambda b, i: (0, qblk(b, i), 0)),
                  pl.BlockSpec((N_PAIRS, 2 * LANES, s_len), lambda b, i: (0, 0, b)),
                  pl.BlockSpec((N_PAIRS, s_len, LANES), lambda b, i: (0, b, 0))],
        out_specs=pl.BlockSpec((N_PAIRS, tq, LANES), lambda b, i: (0, qblk(b, i), 0)),
        out_shape=jax.ShapeDtypeStruct((N_PAIRS, t, LANES), BF16),
        compiler_params=_cparams(("arbitrary", "arbitrary")),
        name="attention",
    )(qn, qp, kcat, v)


def _post_kernel(act_ref, wmix_ref, bmix_ref, xs_ref, xs2_ref, mod_ref, nfw_ref, rwt_ref, rb_ref,
                 sw1_ref, sw3_ref, sw2_ref,
                 h2_out, base_out, idx_out, wt_out, rank_out, cnt_out, cnt_scr,
                 *, pair_major, n_first):
    i = pl.program_id(0)
    tm = xs_ref.shape[0]
    xs_in = jnp.where(i < n_first, xs_ref[...], xs2_ref[...])

    @pl.when(i == 0)
    def _():
        cnt_scr[...] = jnp.zeros_like(cnt_scr)

    if pair_major:
        act = jnp.concatenate([act_ref[j] for j in range(N_PAIRS)], axis=1)
    else:
        act = act_ref[...]
    m = mod_ref[0]
    mix = _dot(act, wmix_ref[...]) + bmix_ref[...]
    xs = xs_in + m[2:3] * mix
    h2 = _rms(xs, nfw_ref[...]) * (1.0 + m[4:5]) + m[3:4]
    _store_row_groups(h2_out, _pack_bf16_pairs(h2))

    h2b = h2.astype(BF16)
    hid = _silu(_dot(h2b, sw1_ref[...])) * _dot(h2b, sw3_ref[...])
    base_out[...] = xs + m[5:6] * _dot(hid.astype(BF16), sw2_ref[...])

    logits = lax.dot_general(rwt_ref[...], h2, (((1,), (1,)), ((), ())),
                             preferred_element_type=F32, precision=lax.Precision.HIGHEST)
    scores = _sigmoid(logits)
    sel = scores + rb_ref[...]
    per_group = N_EXPERTS // N_GROUPS
    neg = jnp.float32(-jnp.inf)
    sub_iota = lax.broadcasted_iota(jnp.int32, (per_group, tm), 0)
    gs_rows = []
    for g in range(N_GROUPS):
        sg = sel[g * per_group:(g + 1) * per_group, :]
        m1 = jnp.max(sg, axis=0, keepdims=True)
        first = jnp.min(jnp.where(sg == m1, sub_iota, per_group), axis=0, keepdims=True)
        m2 = jnp.max(jnp.where(sub_iota == first, neg, sg), axis=0, keepdims=True)
        gs_rows.append(m1 + m2)
    gs = jnp.concatenate(gs_rows, axis=0)
    g_iota = lax.broadcasted_iota(jnp.int32, (N_GROUPS, tm), 0)
    gmask = jnp.zeros((N_GROUPS, tm), F32)
    work = gs
    for _ in range(TOPK_GROUPS):
        mx = jnp.max(work, axis=0, keepdims=True)
        gi = jnp.min(jnp.where(work == mx, g_iota, N_GROUPS), axis=0, keepdims=True)
        pick = g_iota == gi
        gmask = jnp.where(pick, 1.0, gmask)
        work = jnp.where(pick, neg, work)
    emask = jnp.concatenate(
        [jnp.broadcast_to(gmask[g:g + 1, :], (per_group, tm)) for g in range(N_GROUPS)], axis=0)
    masked = jnp.where(emask > 0.5, sel, neg)
    e_iota = lax.broadcasted_iota(jnp.int32, (N_EXPERTS, tm), 0)
    picks, idx_rows, w_rows = [], [], []
    for _ in range(TOP_K):
        mx = jnp.max(masked, axis=0, keepdims=True)
        ei = jnp.min(jnp.where(masked == mx, e_iota, N_EXPERTS), axis=0, keepdims=True)
        pick = e_iota == ei
        picks.append(pick)
        idx_rows.append(ei)
        w_rows.append(jnp.sum(jnp.where(pick, scores, 0.0), axis=0, keepdims=True))
        masked = jnp.where(pick, neg, masked)
    wsum = w_rows[0]
    for r in w_rows[1:]:
        wsum = wsum + r
    wnorm = ROUTED_SCALE / wsum

    member = jnp.zeros((N_EXPERTS, tm), F32)
    for pick in picks:
        member = jnp.where(pick, 1.0, member)
    before = (lax.broadcasted_iota(jnp.int32, (tm, tm), 0)
              < lax.broadcasted_iota(jnp.int32, (tm, tm), 1))
    prefix = _dot(member.astype(BF16), jnp.where(before, 1.0, 0.0).astype(BF16))
    cnt = cnt_scr[...]
    rank_full = cnt[:, 0:1] + prefix
    rank_rows = [jnp.sum(jnp.where(pick, rank_full, 0.0), axis=0, keepdims=True)
                 for pick in picks]
    cnt = cnt + jnp.sum(member, axis=1, keepdims=True)
    cnt_scr[...] = cnt
    cnt_out[...] = cnt

    pad = SUBLANES - TOP_K
    zi = jnp.zeros((pad, tm), jnp.int32)
    zf = jnp.zeros((pad, tm), F32)
    idx_out[...] = jnp.concatenate(idx_rows + [zi], axis=0)
    wt_out[...] = jnp.concatenate([r * wnorm for r in w_rows] + [zf], axis=0)
    rank_out[...] = jnp.concatenate([r.astype(jnp.int32) for r in rank_rows] + [zi], axis=0)


def _post(act, wmix, bmix, xs, xs2, mod, nfw, rwt, rb, sw1, sw3, sw2, *, tm, nt, mrow,
          pair_major):
    d = xs.shape[1]
    t = nt * tm
    g = d // 2 // LANES
    n_first = min(nt, xs.shape[0] // tm)
    full = lambda a: pl.BlockSpec(a.shape, lambda i: (0,) * a.ndim)
    if pair_major:
        act_spec = pl.BlockSpec((N_PAIRS, tm, LANES), lambda i: (0, i, 0))
    else:
        act_spec = pl.BlockSpec((tm, act.shape[1]), lambda i: (i, 0))
    row8 = pl.BlockSpec((SUBLANES, tm), lambda i: (0, i))
    return pl.pallas_call(
        functools.partial(_post_kernel, pair_major=pair_major, n_first=n_first),
        grid=(nt,),
        in_specs=[act_spec, full(wmix), full(bmix),
                  pl.BlockSpec((tm, d), lambda i: (jnp.minimum(i, n_first - 1), 0)),
                  pl.BlockSpec((tm, d), lambda i: (jnp.maximum(i - n_first, 0), 0)),
                  pl.BlockSpec((1, 6, d), lambda i: (mrow(i), 0, 0)),
                  full(nfw), full(rwt), full(rb), full(sw1), full(sw3), full(sw2)],
        out_specs=[pl.BlockSpec((tm * g, LANES), lambda i: (i, 0)),
                   pl.BlockSpec((tm, d), lambda i: (i, 0)),
                   row8, row8, row8,
                   pl.BlockSpec((N_EXPERTS, LANES), lambda i: (0, 0))],
        out_shape=[jax.ShapeDtypeStruct((t * g, LANES), jnp.uint32),
                   jax.ShapeDtypeStruct((t, d), F32),
                   jax.ShapeDtypeStruct((SUBLANES, t), jnp.int32),
                   jax.ShapeDtypeStruct((SUBLANES, t), F32),
                   jax.ShapeDtypeStruct((SUBLANES, t), jnp.int32),
                   jax.ShapeDtypeStruct((N_EXPERTS, LANES), F32)],
        scratch_shapes=[pltpu.VMEM((N_EXPERTS, LANES), F32)],
        compiler_params=_cparams(("arbitrary",)),
        name="post_mixer",
    )(act, wmix, bmix, xs, xs2, mod, nfw, rwt, rb, sw1, sw3, sw2)


def _dispatch_kernel(dest_ref, fill_ref, h2_ref, xg_out, zbuf, sem, zsem, *, t_total, tile, g):
    step = pl.program_id(0)
    base = step * tile
    rows = zbuf.shape[0]
    n_blocks = xg_out.shape[0] // rows

    @pl.when(step == 0)
    def _():
        zbuf[...] = jnp.zeros_like(zbuf)

        def fill_copy(n):
            return pltpu.make_async_copy(
                zbuf, xg_out.at[pl.ds(pl.multiple_of(n * rows, rows), rows)], zsem)

        def start(n, carry):
            @pl.when(fill_ref[n] > 0)
            def _():
                fill_copy(n).start()
            return carry

        def wait(n, carry):
            @pl.when(fill_ref[n] > 0)
            def _():
                fill_copy(n).wait()
            return carry

        lax.fori_loop(0, n_blocks, start, 0)
        lax.fori_loop(0, n_blocks, wait, 0)

    def row_copy(r, slot):
        return pltpu.make_async_copy(h2_ref.at[pl.ds(pl.multiple_of(r * g, g), g)],
                                     xg_out.at[pl.ds(pl.multiple_of(slot * g, g), g)], sem)

    def issue(r, carry):
        for k in range(TOP_K):
            row_copy(r, dest_ref[k * t_total + base + r]).start(priority=k % 2)
        return carry

    lax.fori_loop(0, tile, issue, 0, unroll=ISSUE_UNROLL)

    def drain(r, carry):
        for k in range(TOP_K):
            row_copy(0, 0).wait()
        return carry

    lax.fori_loop(0, tile, drain, 0, unroll=ISSUE_UNROLL)


def _dispatch(dest, fill, h2, n_slots, g):
    t = h2.shape[0] // g
    tile = DISPATCH_TILE
    return pl.pallas_call(
        functools.partial(_dispatch_kernel, t_total=t, tile=tile, g=g),
        grid_spec=pltpu.PrefetchScalarGridSpec(
            num_scalar_prefetch=2,
            grid=(t // tile,),
            in_specs=[pl.BlockSpec((tile * g, LANES), lambda i, dr, fr: (i, 0))],
            out_specs=pl.BlockSpec(memory_space=pl.ANY),
            scratch_shapes=[pltpu.VMEM((EXPERT_ROWS * g, LANES), h2.dtype),
                            pltpu.SemaphoreType.DMA(()), pltpu.SemaphoreType.DMA(())]),
        out_shape=jax.ShapeDtypeStruct((n_slots * g, LANES), h2.dtype),
        compiler_params=_cparams(("arbitrary",)),
        name="moe_dispatch",
    )(dest, fill, h2)


def _expert_kernel(be_ref, nu_ref, first_ref, slot_ref, nxt_ref, x_ref, w1_hbm, w3_hbm, w2_hbm,
                   y_ref, wf1, wf3, wf2, w1b, w3b, w2b, sems, *, g, layer):
    n = pl.program_id(0)
    used = n < nu_ref[0]
    rows = x_ref.shape[0] // g

    def weight_copies(e, s):
        return (pltpu.make_async_copy(w1_hbm.at[layer, e], wf1.at[s], sems.at[s]),
                pltpu.make_async_copy(w3_hbm.at[layer, e], wf3.at[s], sems.at[s]),
                pltpu.make_async_copy(w2_hbm.at[layer, e], wf2.at[s], sems.at[s]))

    @pl.when(n == 0)
    def _():
        for c in weight_copies(be_ref[0], 0):
            c.start()

    @pl.when(first_ref[n] > 0)
    def _():
        s = slot_ref[n]
        for c in weight_copies(be_ref[n], s):
            c.wait()
        w1b[...] = wf1[s].astype(BF16)
        w3b[...] = wf3[s].astype(BF16)
        w2b[...] = wf2[s].astype(BF16)

        @pl.when(nxt_ref[n] >= 0)
        def _():
            for c in weight_copies(nxt_ref[n], 1 - s):
                c.start()

    @pl.when(used)
    def _():
        x = _unpack_bf16_pairs(_load_row_groups(x_ref, rows, g)).astype(BF16)
        hid = _silu(_dot(x, w1b[...])) * _dot(x, w3b[...])
        _store_row_groups(y_ref, _pack_bf16_pairs(_dot(hid.astype(BF16), w2b[...])))

    @pl.when(jnp.logical_not(used))
    def _():
        y_ref[...] = jnp.zeros_like(y_ref)


def _experts(blk_e, n_used, xg, w1, w3, w2, g, layer):
    rows = EXPERT_ROWS * g
    nblk = xg.shape[0] // rows
    _, _, d, e_dim = w1.shape

    pos = jnp.arange(nblk, dtype=jnp.int32)
    first = jnp.logical_or(pos == 0, blk_e != jnp.roll(blk_e, 1))
    slot = (jnp.cumsum(first.astype(jnp.int32)) - 1) % 2
    next_first = lax.cummin(jnp.where(first, pos, nblk), reverse=True)
    next_first = jnp.concatenate([next_first[1:], jnp.full((1,), nblk, jnp.int32)])
    nxt = jnp.where(next_first < nblk, jnp.take(blk_e, jnp.minimum(next_first, nblk - 1)), -1)

    def xmap(n, be, nu, fr, sr, nr):
        return (jnp.minimum(n, nu[0] - 1), 0)

    return pl.pallas_call(
        functools.partial(_expert_kernel, g=g, layer=layer),
        grid_spec=pltpu.PrefetchScalarGridSpec(
            num_scalar_prefetch=5,
            grid=(nblk,),
            in_specs=[pl.BlockSpec((rows, LANES), xmap),
                      pl.BlockSpec(memory_space=pl.ANY),
                      pl.BlockSpec(memory_space=pl.ANY),
                      pl.BlockSpec(memory_space=pl.ANY)],
            out_specs=pl.BlockSpec((rows, LANES), lambda n, be, nu, fr, sr, nr: (n, 0)),
            scratch_shapes=[pltpu.VMEM((2, d, e_dim), F32), pltpu.VMEM((2, d, e_dim), F32),
                            pltpu.VMEM((2, e_dim, d), F32),
                            pltpu.VMEM((d, e_dim), BF16), pltpu.VMEM((d, e_dim), BF16),
                            pltpu.VMEM((e_dim, d), BF16),
                            pltpu.SemaphoreType.DMA((2,))]),
        out_shape=jax.ShapeDtypeStruct(xg.shape, jnp.uint32),
        compiler_params=_cparams(("arbitrary",)),
        name="moe_experts",
    )(blk_e, n_used, first.astype(jnp.int32), slot.astype(jnp.int32), nxt.astype(jnp.int32),
      xg, w1, w3, w2)


def _combine_kernel(dest_ref, y_hbm, base_ref, wt_ref, mod_ref, fnw_ref, *rest,
                    t_total, n_tiles, final_norm, g, n_glu_tiles):
    if n_glu_tiles:
        mod1_ref, nw1_ref, pw_ref, pb_ref, o_ref, glu_ref, buf, sems = rest
    else:
        o_ref, buf, sems = rest
    tm = base_ref.shape[0]
    step = pl.program_id(0)
    cur = step % 2
    half = TOP_K * tm * g
    chunk = SUBLANES
    gate = mod_ref[0][5:6]
    fnw = fnw_ref[...]

    def row_copy(slot, which, k, r):
        dst = pl.multiple_of(which * half + (k * tm + r) * g, g)
        return pltpu.make_async_copy(y_hbm.at[pl.ds(pl.multiple_of(slot * g, g), g)],
                                     buf.at[pl.ds(dst, g)], sems.at[which])

    def issue_chunk(tile, which, r0):
        for dr in range(chunk):
            for k in range(TOP_K):
                slot = dest_ref[k * t_total + tile * tm + r0 + dr]
                row_copy(slot, which, k, r0 + dr).start(priority=k % 2)

    def sum_chunk(r0):
        wt = wt_ref[pl.ds(r0, chunk), :]
        acc = None
        for k in range(TOP_K):
            start = cur * half + (k * tm + r0) * g
            packed = _load_row_groups(buf, chunk, g, start=start)
            term = wt[:, k:k + 1] * _unpack_bf16_pairs(packed)
            acc = term if acc is None else acc + term
        out = base_ref[pl.ds(r0, chunk), :] + gate * acc
        if final_norm:
            out = _rms(out, fnw)
        o_ref[pl.ds(r0, chunk), :] = out

    @pl.when(step == 0)
    def _():
        def first(c, carry):
            issue_chunk(0, 0, pl.multiple_of(c * chunk, chunk))
            return carry
        lax.fori_loop(0, tm // chunk, first, 0)

    def drain(r, carry):
        for k in range(TOP_K):
            row_copy(0, cur, k, 0).wait()
        return carry

    lax.fori_loop(0, tm, drain, 0, unroll=ISSUE_UNROLL)

    @pl.when(step + 1 < n_tiles)
    def _():
        def body(c, carry):
            r0 = pl.multiple_of(c * chunk, chunk)
            sum_chunk(r0)
            issue_chunk(step + 1, 1 - cur, r0)
            return carry
        lax.fori_loop(0, tm // chunk, body, 0, unroll=2)

    @pl.when(step + 1 == n_tiles)
    def _():
        def body(c, carry):
            sum_chunk(pl.multiple_of(c * chunk, chunk))
            return carry
        lax.fori_loop(0, tm // chunk, body, 0)

    if n_glu_tiles:
        @pl.when(step < n_glu_tiles)
        def _():
            _pre_conv_kernel(o_ref, mod1_ref, nw1_ref, pw_ref, pb_ref, glu_ref)


def _combine(dest, yg, base, wt_t, mod, fnw, *, nt, mrow, final_norm, g, glu_args=None):
    d = base.shape[1]
    tm = COMBINE_TILE
    t = nt * tm
    const = lambda a: pl.BlockSpec(a.shape, lambda i, dr: (0,) * a.ndim)
    in_specs = [pl.BlockSpec(memory_space=pl.ANY),
                pl.BlockSpec((tm, d), lambda i, dr: (i, 0)),
                pl.BlockSpec((tm, SUBLANES), lambda i, dr: (i, 0)),
                pl.BlockSpec((1, 6, d), lambda i, dr: (mrow(i), 0, 0)),
                pl.BlockSpec((1, d), lambda i, dr: (0, 0))]
    out_specs = [pl.BlockSpec((tm, d), lambda i, dr: (i, 0))]
    out_shape = [jax.ShapeDtypeStruct((t, d), F32)]
    args = [dest, yg, base, wt_t, mod, fnw]
    n_glu = 0
    if glu_args is not None:
        mod1, mrow1, nw1, pw, pb, n_glu = glu_args
        last = n_glu - 1
        in_specs += [pl.BlockSpec((1, 6, d), lambda i, dr: (mrow1(jnp.minimum(i, last)), 0, 0)),
                     const(nw1), const(pw), const(pb)]
        out_specs.append(pl.BlockSpec((tm, d), lambda i, dr: (jnp.minimum(i, last), 0)))
        out_shape.append(jax.ShapeDtypeStruct((n_glu * tm, d), F32))
        args += [mod1, nw1, pw, pb]
    res = pl.pallas_call(
        functools.partial(_combine_kernel, t_total=base.shape[0], n_tiles=nt,
                          final_norm=final_norm, g=g, n_glu_tiles=n_glu),
        grid_spec=pltpu.PrefetchScalarGridSpec(
            num_scalar_prefetch=1,
            grid=(nt,),
            in_specs=in_specs,
            out_specs=out_specs,
            scratch_shapes=[pltpu.VMEM((2 * TOP_K * tm * g, LANES), jnp.uint32),
                            pltpu.SemaphoreType.DMA((2,))]),
        out_shape=out_shape,
        compiler_params=_cparams(("arbitrary",)),
        name="moe_combine",
    )(*args)
    return res if glu_args is not None else res[0]


def _moe(h2, base, idx8, wt8, rank8, cnt, mod, fnw, w1, w3, w2, *, layer, nt_out, mrow,
         final_norm, glu_args=None):
    g = base.shape[1] // 2 // LANES
    t = h2.shape[0] // g
    rows = EXPERT_ROWS
    counts = cnt[:, 0].astype(jnp.int32)
    padded = (counts + rows - 1) // rows * rows
    pad_end = jnp.cumsum(padded)
    pad_start = pad_end - padded
    nblk = (t * TOP_K + rows - 1) // rows + N_EXPERTS
    e_ids = jnp.arange(N_EXPERTS, dtype=jnp.int32)
    idx = idx8[:TOP_K]
    start_of = jnp.sum(jnp.where(idx[:, :, None] == e_ids, pad_start, 0), axis=-1)
    dest = (start_of + rank8[:TOP_K]).reshape(-1)
    n_used = (pad_end[-1] // rows).astype(jnp.int32)
    blk_start = jnp.arange(nblk, dtype=jnp.int32) * rows
    e_of_blk = jnp.sum((pad_end[None, :] <= blk_start[:, None]).astype(jnp.int32), axis=1)
    valid_end = jnp.sum(jnp.where(e_of_blk[:, None] == e_ids, pad_start + counts, 0), axis=1)
    fill = (blk_start + rows > valid_end).astype(jnp.int32)
    e_last = jnp.max(jnp.where(counts > 0, e_ids, 0))
    blk_e = jnp.minimum(e_of_blk, e_last).astype(jnp.int32)
    xg = _dispatch(dest, fill, h2, nblk * rows, g)
    yg = _experts(blk_e, n_used.reshape(1), xg, w1, w3, w2, g, layer)
    tile_ratio = TOKEN_TILE // COMBINE_TILE
    return _combine(dest, yg, base, wt8.T, mod, fnw, nt=nt_out * tile_ratio,
                    mrow=lambda i: mrow(i // tile_ratio), final_norm=final_norm, g=g,
                    glu_args=glu_args)


def _pre_conv_kernel(xs_ref, mod_ref, nw_ref, w_ref, b_ref, o_ref):
    m = mod_ref[0]
    h = _rms(xs_ref[...], nw_ref[...]) * (1.0 + m[1:2]) + m[0:1]
    u = _dot(h.astype(BF16), w_ref[...]) + b_ref[...]
    d = o_ref.shape[1]
    o_ref[...] = u[:, :d] * _sigmoid(u[:, d:])


def _conv_kernel(prev_ref, cur_ref, next_ref, dw_ref, db_ref, lw_ref, lb_ref, o_ref, win, conv,
                 *, tpb):
    tm, d = cur_ref.shape
    r = pl.program_id(0) % tpb
    halo = CONV_HALO
    pad = CONV_WIDTH // 2
    win[0:halo, :] = jnp.where(r > 0, prev_ref[...], 0.0)
    win[halo:halo + tm, :] = cur_ref[...]
    win[halo + tm:halo + tm + halo, :] = jnp.where(r < tpb - 1, next_ref[...], 0.0)

    rows = 64
    for c in range(d // LANES):
        cs = slice(c * LANES, (c + 1) * LANES)
        taps = dw_ref[:, cs]
        bias = db_ref[:, cs]

        def chunk(q, carry):
            r0 = pl.multiple_of(q * rows, rows)
            slab = win[pl.ds(r0, rows + 2 * halo), cs]
            acc = jnp.zeros((rows, LANES), F32) + bias
            for b in range(SUBLANES):
                part = None
                for j in range(CONV_WIDTH):
                    off = halo - pad + j
                    if off % SUBLANES == b:
                        a = off - b
                        term = taps[j:j + 1, :] * slab[a:a + rows + SUBLANES, :]
                        part = term if part is None else part + term
                acc = acc + part[b:b + rows, :]
            conv[pl.ds(r0, rows), cs] = acc
            return carry

        lax.fori_loop(0, tm // rows, chunk, 0)

    u = conv[...]
    mu = jnp.mean(u, axis=-1, keepdims=True)
    var = jnp.mean(jnp.square(u - mu), axis=-1, keepdims=True)
    y = (u - mu) * lax.rsqrt(var + NORM_EPS) * lw_ref[...] + lb_ref[...]
    o_ref[...] = _silu(y).astype(BF16)


def _conv(glu, dw, db, lw, lb, *, nt, tpb):
    d = glu.shape[1]
    tm = TOKEN_TILE
    hb = tm // CONV_HALO
    n_halo_blocks = glu.shape[0] // CONV_HALO
    full = lambda a: pl.BlockSpec(a.shape, lambda i: (0,) * a.ndim)
    return pl.pallas_call(
        functools.partial(_conv_kernel, tpb=tpb),
        grid=(nt,),
        in_specs=[pl.BlockSpec((CONV_HALO, d), lambda i: (jnp.maximum(i * hb - 1, 0), 0)),
                  pl.BlockSpec((tm, d), lambda i: (i, 0)),
                  pl.BlockSpec((CONV_HALO, d),
                               lambda i: (jnp.minimum((i + 1) * hb, n_halo_blocks - 1), 0)),
                  full(dw), full(db), full(lw), full(lb)],
        out_specs=pl.BlockSpec((tm, d), lambda i: (i, 0)),
        out_shape=jax.ShapeDtypeStruct((nt * tm, d), BF16),
        scratch_shapes=[pltpu.VMEM((tm + 2 * CONV_HALO, d), F32), pltpu.VMEM((tm, d), F32)],
        compiler_params=_cparams(("arbitrary",)),
        name="dw_conv",
    )(glu, glu, glu, dw, db, lw, lb)


def _rope_tables(n_tokens, extra_rows):
    rows = n_tokens // GRID_W
    row = jnp.repeat(jnp.arange(rows, dtype=F32), GRID_W)
    col = jnp.tile(jnp.arange(GRID_W, dtype=F32), rows)
    axis_dim = QK_ROPE_DIM // 2
    inv_freq = ROPE_THETA ** (-jnp.arange(0, axis_dim, 2, dtype=F32) / axis_dim)
    ang_r = row[:, None] * inv_freq
    ang_c = col[:, None] * inv_freq
    ang = jnp.concatenate([ang_r, ang_r, ang_c, ang_c], axis=-1)
    cos = jnp.concatenate([jnp.cos(ang), jnp.ones((extra_rows, QK_ROPE_DIM), F32)], axis=0)
    sin = jnp.concatenate([jnp.sin(ang), jnp.zeros((extra_rows, QK_ROPE_DIM), F32)], axis=0)
    return jnp.tile(cos, (1, N_HEADS)), jnp.tile(sin, (1, N_HEADS))


def kernel(x, c, ctx, c_ctx, mod_w, mod_b, norm_mix_w, norm_ffn_w, mla_wq_a, mla_q_norm, mla_wq_b, mla_wkv_a, mla_kv_norm, mla_wkv_b, mla_wo, conv_pw1_w, conv_pw1_b, conv_dw_w, conv_dw_b, conv_norm_w, conv_norm_b, conv_pw2_w, conv_pw2_b, router_w, router_bias, exp_w1, exp_w3, exp_w2, shared_w1, shared_w3, shared_w2, final_norm_w):
    nb, seq, d = x.shape
    n_ctx = ctx.shape[1]
    tm = TOKEN_TILE
    tpb, cpb = seq // tm, n_ctx // tm
    nx, nc = nb * tpb, nb * cpb
    row = lambda a: a.reshape(1, -1)

    pad_rows = (-(nb + 1)) % SUBLANES
    cvec = jnp.concatenate([c, c_ctx[None, :], jnp.zeros((pad_rows, d), F32)], axis=0)
    mod = _modulation(cvec, mod_w, mod_b).reshape(mod_w.shape[0], cvec.shape[0], 6, d)

    def mrow_all(i):
        return jnp.where(i < nx, i // tpb, nb)

    lat2d, ctx2d = x.reshape(nb * seq, d), ctx.reshape(nb * n_ctx, d)
    wqb =mla_wq_b[0].reshape(Q_LORA_RANK, N_HEADS, QK_NOPE_DIM + QK_ROPE_DIM)
    wqb = jnp.concatenate([wqb[:, :, :QK_NOPE_DIM].reshape(Q_LORA_RANK, -1),
                           wqb[:, :, QK_NOPE_DIM:].reshape(Q_LORA_RANK, -1)], axis=1)
    wkva = mla_wkv_a[0]
    wkva = jnp.concatenate([wkva[:, :KV_LORA_RANK]]
                           + [wkva[:, KV_LORA_RANK:]] * (LANES // QK_ROPE_DIM), axis=1)
    wkvb = mla_wkv_b[0].reshape(KV_LORA_RANK, N_HEADS, QK_NOPE_DIM + V_HEAD_DIM)
    wkvb = jnp.concatenate([wkvb[:, :, :QK_NOPE_DIM].reshape(KV_LORA_RANK, -1),
                            wkvb[:, :, QK_NOPE_DIM:].reshape(KV_LORA_RANK, -1)], axis=1)
    cos, sin = _rope_tables(seq, tm)
    qn, qp, kcat, v = _pre_mla(
        lat2d, ctx2d, mod[0], row(norm_mix_w[0]), mla_wq_a[0].astype(BF16), row(mla_q_norm[0]),
        wqb.astype(BF16), wkva.astype(BF16), row(mla_kv_norm[0]), wkvb.astype(BF16), cos, sin,
        nb=nb, tpb=tpb, cpb=cpb)
    o = _attention(qn, qp, kcat, v, nb=nb, tpb=tpb, cpb=cpb)

    moe_w = (exp_w1, exp_w3, exp_w2)
    post_w = lambda i: (row(norm_ffn_w[i]), router_w[i].T, router_bias[i].reshape(-1, 1),
                        shared_w1[i].astype(BF16), shared_w3[i].astype(BF16),
                        shared_w2[i].astype(BF16))
    ptile = POST_TILE if seq % POST_TILE == 0 and (nb * n_ctx) % POST_TILE == 0 else tm
    ppb = seq // ptile
    h2, base, idx8, wt8, rank8, cnt = _post(
        o, mla_wo[0].astype(BF16), jnp.zeros((1, d), F32), lat2d, ctx2d, mod[0], *post_w(0),
        tm=ptile, nt=(nb * (seq + n_ctx)) // ptile,
        mrow=lambda i: jnp.where(i < nb * ppb, i // ppb, nb), pair_major=True)
    mrow_x = lambda i: i // tpb
    xs, glu = _moe(h2, base, idx8, wt8, rank8, cnt, mod[0], row(final_norm_w), *moe_w,
                   layer=0, nt_out=nx + nc, mrow=mrow_all, final_norm=False,
                   glu_args=(mod[1], mrow_x, row(norm_mix_w[1]), conv_pw1_w[0].astype(BF16),
                             row(conv_pw1_b[0]), nx))
    taps = jnp.concatenate([conv_dw_w[0], jnp.zeros((1, d), F32)], axis=0)
    act = _conv(glu, taps, row(conv_dw_b[0]), row(conv_norm_w[0]), row(conv_norm_b[0]),
                nt=nx, tpb=tpb)
    h2, base, idx8, wt8, rank8, cnt = _post(
        act, conv_pw2_w[0].astype(BF16), row(conv_pw2_b[0]), xs, xs, mod[1],
        *post_w(1), tm=ptile, nt=nb * ppb, mrow=lambda i: i // ppb, pair_major=False)
    out = _moe(h2, base, idx8, wt8, rank8, cnt, mod[1], row(final_norm_w), *moe_w,
               layer=1, nt_out=nx, mrow=mrow_x, final_norm=True)
    return out.reshape(nb, seq, d)
```

```python
import functools
import math

import jax
import jax.numpy as jnp
from jax import lax
from jax.experimental import pallas as pl
from jax.experimental.pallas import tpu as pltpu

N_HEADS = 16
QK_NOPE_DIM = 64
QK_ROPE_DIM = 32
V_HEAD_DIM = 64
Q_LORA_RANK = 512
KV_LORA_RANK = 256
ROPE_THETA = 10000.0
GRID_W = 64
CONV_WIDTH = 31
N_EXPERTS = 64
TOP_K = 6
N_GROUPS = 8
TOPK_GROUPS = 4
ROUTED_SCALE = 2.5
NORM_EPS = 1e-6
LOG2_E = 1.4426950408889634

LANES = 128
SUBLANES = 8
TOKEN_TILE = 256
POST_TILE = 512
EXPERT_ROWS = 512
DISPATCH_TILES = (2048, 1024, 512, 256)
COMBINE_TILES = (512, 256)
ISSUE_UNROLL = 4
CONV_HALO = 16
N_PAIRS = N_HEADS // 2
N_ROPE_GROUPS = N_HEADS // 4
VMEM_LIMIT = 56 * 1024 * 1024

F32 = jnp.float32
BF16 = jnp.bfloat16


def _cparams(sem):
    return pltpu.CompilerParams(dimension_semantics=sem, vmem_limit_bytes=VMEM_LIMIT)


def _rms(x, w):
    return x * lax.rsqrt(jnp.mean(x * x, axis=-1, keepdims=True) + NORM_EPS) * w


def _sigmoid(x):
    return 1.0 / (1.0 + jnp.exp(-x))


def _silu(x):
    return x * _sigmoid(x)


def _dot(a, b):
    return jnp.dot(a, b, preferred_element_type=F32)


def _pack_bf16_pairs(x):
    n = x.shape[1] // 2
    hi = lax.bitcast_convert_type(x[:, :n].astype(BF16).astype(F32), jnp.uint32)
    lo = lax.bitcast_convert_type(x[:, n:].astype(BF16).astype(F32), jnp.uint32)
    return hi | (lo >> 16)


def _unpack_bf16_pairs(p):
    hi = lax.bitcast_convert_type(p & jnp.uint32(0xFFFF0000), F32)
    lo = lax.bitcast_convert_type(p << 16, F32)
    return jnp.concatenate([hi, lo], axis=1)


def _store_row_groups(ref, packed):
    m, n = packed.shape
    g = n // LANES
    for c in range(g):
        ref[pl.ds(c, m, stride=g), :] = packed[:, c * LANES:(c + 1) * LANES]


def _load_row_groups(ref, m, g, start=0):
    return jnp.concatenate([ref[pl.ds(start + c, m, stride=g), :] for c in range(g)], axis=1)


def _mod_kernel(c_ref, w_ref, b_ref, o_ref):
    c = c_ref[...]
    o_ref[0] = jnp.dot(_silu(c), w_ref[0], preferred_element_type=F32,
                       precision=lax.Precision.HIGHEST) + b_ref[0]


def _modulation(cvec, mod_w, mod_b):
    depth, d, n = mod_w.shape
    rows = cvec.shape[0]
    tn = 1536
    return pl.pallas_call(
        _mod_kernel,
        grid=(depth, n // tn),
        in_specs=[pl.BlockSpec((rows, d), lambda l, j: (0, 0)),
                  pl.BlockSpec((1, d, tn), lambda l, j: (l, 0, j)),
                  pl.BlockSpec((1, 1, tn), lambda l, j: (l, 0, j))],
        out_specs=pl.BlockSpec((1, rows, tn), lambda l, j: (l, 0, j)),
        out_shape=jax.ShapeDtypeStruct((depth, rows, n), F32),
        compiler_params=_cparams(("arbitrary", "arbitrary")),
        name="modulation",
    )(cvec, mod_w, mod_b.reshape(depth, 1, n))


def _rope(v, cos, sin):
    n = v.shape[1]
    lane = lax.broadcasted_iota(jnp.int32, v.shape, 1)
    up = pltpu.roll(v, 8, axis=1)
    dn = pltpu.roll(v, n - 8, axis=1)
    rot = jnp.where(lane % 16 < 8, -dn, up)
    return v * cos + rot * sin


def _pre_mla_kernel(lat_ref, ctx_ref, mod_ref, nw_ref, wqa_ref, qn_ref, wqb_ref, wkva_ref, kvn_ref,
                    wkvb_ref, cos_ref, sin_ref, qn_out, qp_out, k_out, v_out, *, n_lat_tiles):
    x = jnp.where(pl.program_id(0) < n_lat_tiles, lat_ref[...], ctx_ref[...])
    m = mod_ref[0]
    h = _rms(x, nw_ref[...]) * (1.0 + m[1:2]) + m[0:1]
    hb = h.astype(BF16)
    cos = cos_ref[...]
    sin = sin_ref[...]

    qa = _rms(_dot(hb, wqa_ref[...]), qn_ref[...])
    scale = (QK_NOPE_DIM + QK_ROPE_DIM) ** -0.5 * LOG2_E
    q = _dot(qa.astype(BF16), wqb_ref[...]) * scale
    d_nope = N_HEADS * QK_NOPE_DIM
    q_pe = _rope(q[:, d_nope:], cos, sin)

    kva = _dot(hb, wkva_ref[...])
    kpe4_t = _rope(kva[:, KV_LORA_RANK:], cos[:, :LANES], sin[:, :LANES]).T.astype(BF16)
    ckv = _rms(kva[:, :KV_LORA_RANK], kvn_ref[...])
    kv = _dot(ckv.astype(BF16), wkvb_ref[...])

    for j in range(N_PAIRS):
        sl = slice(j * LANES, (j + 1) * LANES)
        qn_out[j] = q[:, sl].astype(BF16)
        k_out[j, :LANES, :] = kv[:, sl].T.astype(BF16)
        k_out[j, LANES:, :] = kpe4_t
        v_out[j] = kv[:, d_nope + j * LANES:d_nope + (j + 1) * LANES].astype(BF16)
    for g in range(N_ROPE_GROUPS):
        qp_out[g] = q_pe[:, g * LANES:(g + 1) * LANES].astype(BF16)


def _pre_mla(lat, ctx, mod, nw, wqa, qn, wqb, wkva, kvn, wkvb, cos, sin, *, nb, tpb, cpb):
    d = lat.shape[1]
    t = lat.shape[0] + ctx.shape[0]
    tm = TOKEN_TILE
    nx = nb * tpb
    spb = tpb + cpb

    def mrow(i):
        return jnp.where(i < nx, i // tpb, nb)

    def kvblk(i):
        ic = i - nx
        return jnp.where(i < nx, (i // tpb) * spb + cpb + i % tpb, (ic // cpb) * spb + ic % cpb)

    def ropeblk(i):
        return jnp.where(i < nx, i % tpb, tpb)

    full = lambda a: pl.BlockSpec(a.shape, lambda i: (0,) * a.ndim)
    s_rows = nb * spb * tm
    return pl.pallas_call(
        functools.partial(_pre_mla_kernel, n_lat_tiles=nx),
        grid=(t // tm,),
        in_specs=[pl.BlockSpec((tm, d), lambda i: (jnp.minimum(i, nx - 1), 0)),
                  pl.BlockSpec((tm, d), lambda i: (jnp.maximum(i - nx, 0), 0)),
                  pl.BlockSpec((1, 6, d), lambda i: (mrow(i), 0, 0)),
                  full(nw), full(wqa), full(qn), full(wqb), full(wkva), full(kvn), full(wkvb),
                  pl.BlockSpec((tm, cos.shape[1]), lambda i: (ropeblk(i), 0)),
                  pl.BlockSpec((tm, sin.shape[1]), lambda i: (ropeblk(i), 0))],
        out_specs=[pl.BlockSpec((N_PAIRS, tm, LANES), lambda i: (0, i, 0)),
                   pl.BlockSpec((N_ROPE_GROUPS, tm, LANES), lambda i: (0, i, 0)),
                   pl.BlockSpec((N_PAIRS, 2 * LANES, tm), lambda i: (0, 0, kvblk(i))),
                   pl.BlockSpec((N_PAIRS, tm, LANES), lambda i: (0, kvblk(i), 0))],
        out_shape=[jax.ShapeDtypeStruct((N_PAIRS, t, LANES), BF16),
                   jax.ShapeDtypeStruct((N_ROPE_GROUPS, t, LANES), BF16),
                   jax.ShapeDtypeStruct((N_PAIRS, 2 * LANES, s_rows), BF16),
                   jax.ShapeDtypeStruct((N_PAIRS, s_rows, LANES), BF16)],
        compiler_params=_cparams(("arbitrary",)),
        name="pre_mla",
    )(lat, ctx, mod, nw, wqa, qn, wqb, wkva, kvn, wkvb, cos, sin)


def _attn_tile(qn_ref, qp_ref, k_ref, v_ref, o_ref, s_len):
    tq = qn_ref.shape[1]
    lane = lax.broadcasted_iota(jnp.int32, (tq, LANES), 1)

    def pair(j, carry):
        qn = qn_ref[j]
        qp = qp_ref[j // 2]
        k = k_ref[j, :, :s_len]
        v = v_ref[j, :s_len, :]
        outs = []
        for e in range(2):
            qn_m = jnp.where(lane // QK_NOPE_DIM == e, qn, jnp.zeros_like(qn))
            qp_m = jnp.where(lane // QK_ROPE_DIM == 2 * (j % 2) + e, qp, jnp.zeros_like(qp))
            lhs = jnp.concatenate([qn_m, qp_m], axis=1)
            s = _dot(lhs, k)
            p = jnp.exp2(s - jnp.max(s, axis=-1, keepdims=True))
            l = jnp.sum(p, axis=-1, keepdims=True)
            outs.append(_dot(p.astype(BF16), v) * (1.0 / l))
        o_ref[j] = jnp.where(lane < V_HEAD_DIM, outs[0], outs[1]).astype(BF16)
        return carry

    lax.fori_loop(0, N_PAIRS, pair, 0, unroll=8)


def _attn_kernel(qn_ref, qp_ref, k_ref, v_ref, o_ref, *, tpb, n_ctx):
    step = pl.program_id(1)

    @pl.when(step < tpb)
    def _():
        _attn_tile(qn_ref, qp_ref, k_ref, v_ref, o_ref, v_ref.shape[1])

    @pl.when(step >= tpb)
    def _():
        _attn_tile(qn_ref, qp_ref, k_ref, v_ref, o_ref, n_ctx)


def _attention(qn, qp, kcat, v, *, nb, tpb, cpb):
    t = qn.shape[1]
    tq = TOKEN_TILE
    nx = nb * tpb
    s_len = (tpb + cpb) * tq

    def qblk(b, i):
        return jnp.where(i < tpb, b * tpb + i, nx + b * cpb + (i - tpb))

    return pl.pallas_call(
        functools.partial(_attn_kernel, tpb=tpb, n_ctx=cpb * tq),
        grid=(nb, tpb + cpb),
        in_specs=[pl.BlockSpec((N_PAIRS, tq, LANES), lambda b, i: (0, qblk(b, i), 0)),
                  pl.BlockSpec((N_ROPE_GROUPS, tq, LANES), lambda b, i: (0, qblk(b, i), 0)),
                  pl.BlockSpec((N_PAIRS, 2 * LANES, s_len), lambda b, i: (0, 0, b)),
                  pl.BlockSpec((N_PAIRS, s_len, LANES), lambda b, i: (0, b, 0))],
        out_specs=pl.BlockSpec((N_PAIRS, tq, LANES), lambda b, i: (0, qblk(b, i), 0)),
        out_shape=jax.ShapeDtypeStruct((N_PAIRS, t, LANES), BF16),
        compiler_params=_cparams(("arbitrary", "arbitrary")),
        name="attention",
    )(qn, qp, kcat, v)


def _post_kernel(act_ref, wmix_ref, bmix_ref, xs_ref, xs2_ref, mod_ref, nfw_ref, rwt_ref, rb_ref,
                 sw1_ref, sw3_ref, sw2_ref,
                 h2_out, base_out, idx_out, wt_out, rank_out, cnt_out, cnt_scr,
                 *, pair_major, n_first):
    i = pl.program_id(0)
    tm = xs_ref.shape[0]
    xs_in = jnp.where(i < n_first, xs_ref[...], xs2_ref[...])

    @pl.when(i == 0)
    def _():
        cnt_scr[...] = jnp.zeros_like(cnt_scr)

    if pair_major:
        act = jnp.concatenate([act_ref[j] for j in range(N_PAIRS)], axis=1)
    else:
        act = act_ref[...]
    m = mod_ref[0]
    mix = _dot(act, wmix_ref[...]) + bmix_ref[...]
    xs = xs_in + m[2:3] * mix
    h2 = _rms(xs, nfw_ref[...]) * (1.0 + m[4:5]) + m[3:4]
    _store_row_groups(h2_out, _pack_bf16_pairs(h2))

    h2b = h2.astype(BF16)
    hid = _silu(_dot(h2b, sw1_ref[...])) * _dot(h2b, sw3_ref[...])
    base_out[...] = xs + m[5:6] * _dot(hid.astype(BF16), sw2_ref[...])

    logits = lax.dot_general(rwt_ref[...], h2, (((1,), (1,)), ((), ())),
                             preferred_element_type=F32, precision=lax.Precision.HIGHEST)
    scores = _sigmoid(logits)
    sel = scores + rb_ref[...]
    per_group = N_EXPERTS // N_GROUPS
    neg = jnp.float32(-jnp.inf)
    sub_iota = lax.broadcasted_iota(jnp.int32, (per_group, tm), 0)
    gs_rows = []
    for g in range(N_GROUPS):
        sg = sel[g * per_group:(g + 1) * per_group, :]
        m1 = jnp.max(sg, axis=0, keepdims=True)
        first = jnp.min(jnp.where(sg == m1, sub_iota, per_group), axis=0, keepdims=True)
        m2 = jnp.max(jnp.where(sub_iota == first, neg, sg), axis=0, keepdims=True)
        gs_rows.append(m1 + m2)
    gs = jnp.concatenate(gs_rows, axis=0)
    g_iota = lax.broadcasted_iota(jnp.int32, (N_GROUPS, tm), 0)
    gmask = jnp.zeros((N_GROUPS, tm), F32)
    work = gs
    for _ in range(TOPK_GROUPS):
        mx = jnp.max(work, axis=0, keepdims=True)
        gi = jnp.min(jnp.where(work == mx, g_iota, N_GROUPS), axis=0, keepdims=True)
        pick = g_iota == gi
        gmask = jnp.where(pick, 1.0, gmask)
        work = jnp.where(pick, neg, work)
    emask = jnp.concatenate(
        [jnp.broadcast_to(gmask[g:g + 1, :], (per_group, tm)) for g in range(N_GROUPS)], axis=0)
    masked = jnp.where(emask > 0.5, sel, neg)
    e_iota = lax.broadcasted_iota(jnp.int32, (N_EXPERTS, tm), 0)
    picks, idx_rows, w_rows = [], [], []
    for _ in range(TOP_K):
        mx = jnp.max(masked, axis=0, keepdims=True)
        ei = jnp.min(jnp.where(masked == mx, e_iota, N_EXPERTS), axis=0, keepdims=True)
        pick = e_iota == ei
        picks.append(pick)
        idx_rows.append(ei)
        w_rows.append(jnp.sum(jnp.where(pick, scores, 0.0), axis=0, keepdims=True))
        masked = jnp.where(pick, neg, masked)
    wsum = w_rows[0]
    for r in w_rows[1:]:
        wsum = wsum + r
    wnorm = ROUTED_SCALE / wsum

    member = jnp.zeros((N_EXPERTS, tm), F32)
    for pick in picks:
        member = jnp.where(pick, 1.0, member)
    before = (lax.broadcasted_iota(jnp.int32, (tm, tm), 0)
              < lax.broadcasted_iota(jnp.int32, (tm, tm), 1))
    prefix = _dot(member.astype(BF16), jnp.where(before, 1.0, 0.0).astype(BF16))
    cnt = cnt_scr[...]
    rank_full = cnt[:, 0:1] + prefix
    rank_rows = [jnp.sum(jnp.where(pick, rank_full, 0.0), axis=0, keepdims=True)
                 for pick in picks]
    cnt = cnt + jnp.sum(member, axis=1, keepdims=True)
    cnt_scr[...] = cnt
    cnt_out[...] = cnt

    pad = SUBLANES - TOP_K
    zi = jnp.zeros((pad, tm), jnp.int32)
    zf = jnp.zeros((pad, tm), F32)
    idx_out[...] = jnp.concatenate(idx_rows + [zi], axis=0)
    wt_out[...] = jnp.concatenate([r * wnorm for r in w_rows] + [zf], axis=0)
    rank_out[...] = jnp.concatenate([r.astype(jnp.int32) for r in rank_rows] + [zi], axis=0)


def _post(act, wmix, bmix, xs, xs2, mod, nfw, rwt, rb, sw1, sw3, sw2, *, tm, nt, mrow,
          pair_major):
    d = xs.shape[1]
    t = nt * tm
    g = d // 2 // LANES
    n_first = min(nt, xs.shape[0] // tm)
    full = lambda a: pl.BlockSpec(a.shape, lambda i: (0,) * a.ndim)
    if pair_major:
        act_spec = pl.BlockSpec((N_PAIRS, tm, LANES), lambda i: (0, i, 0))
    else:
        act_spec = pl.BlockSpec((tm, act.shape[1]), lambda i: (i, 0))
    row8 = pl.BlockSpec((SUBLANES, tm), lambda i: (0, i))
    return pl.pallas_call(
        functools.partial(_post_kernel, pair_major=pair_major, n_first=n_first),
        grid=(nt,),
        in_specs=[act_spec, full(wmix), full(bmix),
                  pl.BlockSpec((tm, d), lambda i: (jnp.minimum(i, n_first - 1), 0)),
                  pl.BlockSpec((tm, d), lambda i: (jnp.maximum(i - n_first, 0), 0)),
                  pl.BlockSpec((1, 6, d), lambda i: (mrow(i), 0, 0)),
                  full(nfw), full(rwt), full(rb), full(sw1), full(sw3), full(sw2)],
        out_specs=[pl.BlockSpec((tm * g, LANES), lambda i: (i, 0)),
                   pl.BlockSpec((tm, d), lambda i: (i, 0)),
                   row8, row8, row8,
                   pl.BlockSpec((N_EXPERTS, LANES), lambda i: (0, 0))],
        out_shape=[jax.ShapeDtypeStruct((t * g, LANES), jnp.uint32),
                   jax.ShapeDtypeStruct((t, d), F32),
                   jax.ShapeDtypeStruct((SUBLANES, t), jnp.int32),
                   jax.ShapeDtypeStruct((SUBLANES, t), F32),
                   jax.ShapeDtypeStruct((SUBLANES, t), jnp.int32),
                   jax.ShapeDtypeStruct((N_EXPERTS, LANES), F32)],
        scratch_shapes=[pltpu.VMEM((N_EXPERTS, LANES), F32)],
        compiler_params=_cparams(("arbitrary",)),
        name="post_mixer",
    )(act, wmix, bmix, xs, xs2, mod, nfw, rwt, rb, sw1, sw3, sw2)


def _dispatch_kernel(dest_ref, fill_ref, h2_ref, xg_out, zbuf, sem, zsem, *, t_total, tile, g):
    step = pl.program_id(0)
    base = step * tile
    rows = zbuf.shape[0]
    n_blocks = xg_out.shape[0] // rows

    @pl.when(step == 0)
    def _():
        zbuf[...] = jnp.zeros_like(zbuf)

        def fill_copy(n):
            return pltpu.make_async_copy(
                zbuf, xg_out.at[pl.ds(pl.multiple_of(n * rows, rows), rows)], zsem)

        def start(n, carry):
            @pl.when(fill_ref[n] > 0)
            def _():
                fill_copy(n).start()
            return carry

        def wait(n, carry):
            @pl.when(fill_ref[n] > 0)
            def _():
                fill_copy(n).wait()
            return carry

        lax.fori_loop(0, n_blocks, start, 0)
        lax.fori_loop(0, n_blocks, wait, 0)

    def row_copy(r, slot):
        return pltpu.make_async_copy(h2_ref.at[pl.ds(pl.multiple_of(r * g, g), g)],
                                     xg_out.at[pl.ds(pl.multiple_of(slot * g, g), g)], sem)

    def issue(r, carry):
        for k in range(TOP_K):
            row_copy(r, dest_ref[k * t_total + base + r]).start(priority=k % 2)
        return carry

    lax.fori_loop(0, tile, issue, 0, unroll=ISSUE_UNROLL)

    def drain(r, carry):
        for k in range(TOP_K):
            row_copy(0, 0).wait()
        return carry

    lax.fori_loop(0, tile, drain, 0, unroll=ISSUE_UNROLL)


def _dispatch(dest, fill, h2, n_slots, g):
    t = h2.shape[0] // g
    tile = next(c for c in DISPATCH_TILES if t % c == 0)
    return pl.pallas_call(
        functools.partial(_dispatch_kernel, t_total=t, tile=tile, g=g),
        grid_spec=pltpu.PrefetchScalarGridSpec(
            num_scalar_prefetch=2,
            grid=(t // tile,),
            in_specs=[pl.BlockSpec((tile * g, LANES), lambda i, dr, fr: (i, 0))],
            out_specs=pl.BlockSpec(memory_space=pl.ANY),
            scratch_shapes=[pltpu.VMEM((EXPERT_ROWS * g, LANES), h2.dtype),
                            pltpu.SemaphoreType.DMA(()), pltpu.SemaphoreType.DMA(())]),
        out_shape=jax.ShapeDtypeStruct((n_slots * g, LANES), h2.dtype),
        compiler_params=_cparams(("arbitrary",)),
        name="moe_dispatch",
    )(dest, fill, h2)


def _expert_kernel(be_ref, nu_ref, first_ref, slot_ref, nxt_ref, x_ref, w1_hbm, w3_hbm, w2_hbm,
                   y_ref, wf1, wf3, wf2, w1b, w3b, w2b, sems, *, g, layer):
    n = pl.program_id(0)
    used = n < nu_ref[0]
    rows = x_ref.shape[0] // g

    def weight_copies(e, s):
        return (pltpu.make_async_copy(w1_hbm.at[layer, e], wf1.at[s], sems.at[s]),
                pltpu.make_async_copy(w3_hbm.at[layer, e], wf3.at[s], sems.at[s]),
                pltpu.make_async_copy(w2_hbm.at[layer, e], wf2.at[s], sems.at[s]))

    @pl.when(n == 0)
    def _():
        for c in weight_copies(be_ref[0], 0):
            c.start()

    @pl.when(first_ref[n] > 0)
    def _():
        s = slot_ref[n]
        for c in weight_copies(be_ref[n], s):
            c.wait()
        w1b[...] = wf1[s].astype(BF16)
        w3b[...] = wf3[s].astype(BF16)
        w2b[...] = wf2[s].astype(BF16)

        @pl.when(nxt_ref[n] >= 0)
        def _():
            for c in weight_copies(nxt_ref[n], 1 - s):
                c.start()

    @pl.when(used)
    def _():
        x = _unpack_bf16_pairs(_load_row_groups(x_ref, rows, g)).astype(BF16)
        hid = _silu(_dot(x, w1b[...])) * _dot(x, w3b[...])
        _store_row_groups(y_ref, _pack_bf16_pairs(_dot(hid.astype(BF16), w2b[...])))

    @pl.when(jnp.logical_not(used))
    def _():
        y_ref[...] = jnp.zeros_like(y_ref)


def _experts(blk_e, n_used, xg, w1, w3, w2, g, layer):
    rows = EXPERT_ROWS * g
    nblk = xg.shape[0] // rows
    _, _, d, e_dim = w1.shape

    pos = jnp.arange(nblk, dtype=jnp.int32)
    first = jnp.logical_or(pos == 0, blk_e != jnp.roll(blk_e, 1))
    slot = (jnp.cumsum(first.astype(jnp.int32)) - 1) % 2
    next_first = lax.cummin(jnp.where(first, pos, nblk), reverse=True)
    next_first = jnp.concatenate([next_first[1:], jnp.full((1,), nblk, jnp.int32)])
    nxt = jnp.where(next_first < nblk, jnp.take(blk_e, jnp.minimum(next_first, nblk - 1)), -1)

    def xmap(n, be, nu, fr, sr, nr):
        return (jnp.minimum(n, nu[0] - 1), 0)

    return pl.pallas_call(
        functools.partial(_expert_kernel, g=g, layer=layer),
        grid_spec=pltpu.PrefetchScalarGridSpec(
            num_scalar_prefetch=5,
            grid=(nblk,),
            in_specs=[pl.BlockSpec((rows, LANES), xmap),
                      pl.BlockSpec(memory_space=pl.ANY),
                      pl.BlockSpec(memory_space=pl.ANY),
                      pl.BlockSpec(memory_space=pl.ANY)],
            out_specs=pl.BlockSpec((rows, LANES), lambda n, be, nu, fr, sr, nr: (n, 0)),
            scratch_shapes=[pltpu.VMEM((2, d, e_dim), F32), pltpu.VMEM((2, d, e_dim), F32),
                            pltpu.VMEM((2, e_dim, d), F32),
                            pltpu.VMEM((d, e_dim), BF16), pltpu.VMEM((d, e_dim), BF16),
                            pltpu.VMEM((e_dim, d), BF16),
                            pltpu.SemaphoreType.DMA((2,))]),
        out_shape=jax.ShapeDtypeStruct(xg.shape, jnp.uint32),
        compiler_params=_cparams(("arbitrary",)),
        name="moe_experts",
    )(blk_e, n_used, first.astype(jnp.int32), slot.astype(jnp.int32), nxt.astype(jnp.int32),
      xg, w1, w3, w2)


def _combine_kernel(dest_ref, y_hbm, base_ref, wt_ref, mod_ref, fnw_ref, *rest,
                    t_total, n_tiles, final_norm, g, n_glu_tiles):
    if n_glu_tiles:
        mod1_ref, nw1_ref, pw_ref, pb_ref, o_ref, glu_ref, buf, sems = rest
    else:
        o_ref, buf, sems = rest
    tm = base_ref.shape[0]
    step = pl.program_id(0)
    cur = step % 2
    half = TOP_K * tm * g
    chunk = SUBLANES
    gate = mod_ref[0][5:6]
    fnw = fnw_ref[...]

    def row_copy(slot, which, k, r):
        dst = pl.multiple_of(which * half + (k * tm + r) * g, g)
        return pltpu.make_async_copy(y_hbm.at[pl.ds(pl.multiple_of(slot * g, g), g)],
                                     buf.at[pl.ds(dst, g)], sems.at[which])

    def issue_chunk(tile, which, r0):
        for dr in range(chunk):
            for k in range(TOP_K):
                slot = dest_ref[k * t_total + tile * tm + r0 + dr]
                row_copy(slot, which, k, r0 + dr).start(priority=k % 2)

    def sum_chunk(r0):
        wt = wt_ref[pl.ds(r0, chunk), :]
        acc = None
        for k in range(TOP_K):
            start = cur * half + (k * tm + r0) * g
            packed = _load_row_groups(buf, chunk, g, start=start)
            term = wt[:, k:k + 1] * _unpack_bf16_pairs(packed)
            acc = term if acc is None else acc + term
        out = base_ref[pl.ds(r0, chunk), :] + gate * acc
        if final_norm:
            out = _rms(out, fnw)
        o_ref[pl.ds(r0, chunk), :] = out

    @pl.when(step == 0)
    def _():
        def first(c, carry):
            issue_chunk(0, 0, pl.multiple_of(c * chunk, chunk))
            return carry
        lax.fori_loop(0, tm // chunk, first, 0)

    def drain(r, carry):
        for k in range(TOP_K):
            row_copy(0, cur, k, 0).wait()
        return carry

    lax.fori_loop(0, tm, drain, 0, unroll=ISSUE_UNROLL)

    @pl.when(step + 1 < n_tiles)
    def _():
        def body(c, carry):
            r0 = pl.multiple_of(c * chunk, chunk)
            sum_chunk(r0)
            issue_chunk(step + 1, 1 - cur, r0)
            return carry
        lax.fori_loop(0, tm // chunk, body, 0, unroll=2)

    @pl.when(step + 1 == n_tiles)
    def _():
        def body(c, carry):
            sum_chunk(pl.multiple_of(c * chunk, chunk))
            return carry
        lax.fori_loop(0, tm // chunk, body, 0)

    if n_glu_tiles:
        @pl.when(step < n_glu_tiles)
        def _():
            _pre_conv_kernel(o_ref, mod1_ref, nw1_ref, pw_ref, pb_ref, glu_ref)


def _combine(dest, yg, base, wt_t, mod, fnw, *, tm, n_out, mod_row, final_norm, g,
             glu_args=None):
    d = base.shape[1]
    nt = n_out // tm
    t = nt * tm
    const = lambda a: pl.BlockSpec(a.shape, lambda i, dr: (0,) * a.ndim)
    in_specs = [pl.BlockSpec(memory_space=pl.ANY),
                pl.BlockSpec((tm, d), lambda i, dr: (i, 0)),
                pl.BlockSpec((tm, SUBLANES), lambda i, dr: (i, 0)),
                pl.BlockSpec((1, 6, d), lambda i, dr: (mod_row(i * tm), 0, 0)),
                pl.BlockSpec((1, d), lambda i, dr: (0, 0))]
    out_specs = [pl.BlockSpec((tm, d), lambda i, dr: (i, 0))]
    out_shape = [jax.ShapeDtypeStruct((t, d), F32)]
    args = [dest, yg, base, wt_t, mod, fnw]
    n_glu = 0
    if glu_args is not None:
        mod1, mod_row1, nw1, pw, pb, n_glu_tokens = glu_args
        n_glu = n_glu_tokens // tm
        last = n_glu - 1
        in_specs += [pl.BlockSpec((1, 6, d),
                                  lambda i, dr: (mod_row1(jnp.minimum(i, last) * tm), 0, 0)),
                     const(nw1), const(pw), const(pb)]
        out_specs.append(pl.BlockSpec((tm, d), lambda i, dr: (jnp.minimum(i, last), 0)))
        out_shape.append(jax.ShapeDtypeStruct((n_glu * tm, d), F32))
        args += [mod1, nw1, pw, pb]
    res = pl.pallas_call(
        functools.partial(_combine_kernel, t_total=base.shape[0], n_tiles=nt,
                          final_norm=final_norm, g=g, n_glu_tiles=n_glu),
        grid_spec=pltpu.PrefetchScalarGridSpec(
            num_scalar_prefetch=1,
            grid=(nt,),
            in_specs=in_specs,
            out_specs=out_specs,
            scratch_shapes=[pltpu.VMEM((2 * TOP_K * tm * g, LANES), jnp.uint32),
                            pltpu.SemaphoreType.DMA((2,))]),
        out_shape=out_shape,
        compiler_params=_cparams(("arbitrary",)),
        name="moe_combine",
    )(*args)
    return res if glu_args is not None else res[0]


def _moe(h2, base, idx8, wt8, rank8, cnt, mod, fnw, w1, w3, w2, *, layer, n_out, mod_row,
         tile_quantum, final_norm, glu_args=None):
    g = base.shape[1] // 2 // LANES
    t = h2.shape[0] // g
    rows = EXPERT_ROWS
    counts = cnt[:, 0].astype(jnp.int32)
    padded = (counts + rows - 1) // rows * rows
    pad_end = jnp.cumsum(padded)
    pad_start = pad_end - padded
    nblk = (t * TOP_K + rows - 1) // rows + N_EXPERTS
    e_ids = jnp.arange(N_EXPERTS, dtype=jnp.int32)
    idx = idx8[:TOP_K]
    start_of = jnp.sum(jnp.where(idx[:, :, None] == e_ids, pad_start, 0), axis=-1)
    dest = (start_of + rank8[:TOP_K]).reshape(-1)
    n_used = (pad_end[-1] // rows).astype(jnp.int32)
    blk_start = jnp.arange(nblk, dtype=jnp.int32) * rows
    e_of_blk = jnp.sum((pad_end[None, :] <= blk_start[:, None]).astype(jnp.int32), axis=1)
    valid_end = jnp.sum(jnp.where(e_of_blk[:, None] == e_ids, pad_start + counts, 0), axis=1)
    fill = (blk_start + rows > valid_end).astype(jnp.int32)
    e_last = jnp.max(jnp.where(counts > 0, e_ids, 0))
    blk_e = jnp.minimum(e_of_blk, e_last).astype(jnp.int32)
    xg = _dispatch(dest, fill, h2, nblk * rows, g)
    yg = _experts(blk_e, n_used.reshape(1), xg, w1, w3, w2, g, layer)
    tm = next(c for c in COMBINE_TILES if tile_quantum % c == 0 and n_out % c == 0)
    return _combine(dest, yg, base, wt8.T, mod, fnw, tm=tm, n_out=n_out, mod_row=mod_row,
                    final_norm=final_norm, g=g, glu_args=glu_args)


def _pre_conv_kernel(xs_ref, mod_ref, nw_ref, w_ref, b_ref, o_ref):
    m = mod_ref[0]
    h = _rms(xs_ref[...], nw_ref[...]) * (1.0 + m[1:2]) + m[0:1]
    u = _dot(h.astype(BF16), w_ref[...]) + b_ref[...]
    d = o_ref.shape[1]
    o_ref[...] = u[:, :d] * _sigmoid(u[:, d:])


def _conv_kernel(prev_ref, cur_ref, next_ref, dw_ref, db_ref, lw_ref, lb_ref, o_ref, win, conv,
                 *, tpb):
    tm, d = cur_ref.shape
    r = pl.program_id(0) % tpb
    halo = CONV_HALO
    pad = CONV_WIDTH // 2
    win[0:halo, :] = jnp.where(r > 0, prev_ref[...], 0.0)
    win[halo:halo + tm, :] = cur_ref[...]
    win[halo + tm:halo + tm + halo, :] = jnp.where(r < tpb - 1, next_ref[...], 0.0)

    rows = 64
    for c in range(d // LANES):
        cs = slice(c * LANES, (c + 1) * LANES)
        taps = dw_ref[:, cs]
        bias = db_ref[:, cs]

        def chunk(q, carry):
            r0 = pl.multiple_of(q * rows, rows)
            slab = win[pl.ds(r0, rows + 2 * halo), cs]
            acc = jnp.zeros((rows, LANES), F32) + bias
            for b in range(SUBLANES):
                part = None
                for j in range(CONV_WIDTH):
                    off = halo - pad + j
                    if off % SUBLANES == b:
                        a = off - b
                        term = taps[j:j + 1, :] * slab[a:a + rows + SUBLANES, :]
                        part = term if part is None else part + term
                acc = acc + part[b:b + rows, :]
            conv[pl.ds(r0, rows), cs] = acc
            return carry

        lax.fori_loop(0, tm // rows, chunk, 0)

    u = conv[...]
    mu = jnp.mean(u, axis=-1, keepdims=True)
    var = jnp.mean(jnp.square(u - mu), axis=-1, keepdims=True)
    y = (u - mu) * lax.rsqrt(var + NORM_EPS) * lw_ref[...] + lb_ref[...]
    o_ref[...] = _silu(y).astype(BF16)


def _conv(glu, dw, db, lw, lb, *, nt, tpb):
    d = glu.shape[1]
    tm = TOKEN_TILE
    hb = tm // CONV_HALO
    n_halo_blocks = glu.shape[0] // CONV_HALO
    full = lambda a: pl.BlockSpec(a.shape, lambda i: (0,) * a.ndim)
    return pl.pallas_call(
        functools.partial(_conv_kernel, tpb=tpb),
        grid=(nt,),
        in_specs=[pl.BlockSpec((CONV_HALO, d), lambda i: (jnp.maximum(i * hb - 1, 0), 0)),
                  pl.BlockSpec((tm, d), lambda i: (i, 0)),
                  pl.BlockSpec((CONV_HALO, d),
                               lambda i: (jnp.minimum((i + 1) * hb, n_halo_blocks - 1), 0)),
                  full(dw), full(db), full(lw), full(lb)],
        out_specs=pl.BlockSpec((tm, d), lambda i: (i, 0)),
        out_shape=jax.ShapeDtypeStruct((nt * tm, d), BF16),
        scratch_shapes=[pltpu.VMEM((tm + 2 * CONV_HALO, d), F32), pltpu.VMEM((tm, d), F32)],
        compiler_params=_cparams(("arbitrary",)),
        name="dw_conv",
    )(glu, glu, glu, dw, db, lw, lb)


def _rope_tables(n_tokens, extra_rows):
    rows = n_tokens // GRID_W
    row = jnp.repeat(jnp.arange(rows, dtype=F32), GRID_W)
    col = jnp.tile(jnp.arange(GRID_W, dtype=F32), rows)
    axis_dim = QK_ROPE_DIM // 2
    inv_freq = ROPE_THETA ** (-jnp.arange(0, axis_dim, 2, dtype=F32) / axis_dim)
    ang_r = row[:, None] * inv_freq
    ang_c = col[:, None] * inv_freq
    ang = jnp.concatenate([ang_r, ang_r, ang_c, ang_c], axis=-1)
    cos = jnp.concatenate([jnp.cos(ang), jnp.ones((extra_rows, QK_ROPE_DIM), F32)], axis=0)
    sin = jnp.concatenate([jnp.sin(ang), jnp.zeros((extra_rows, QK_ROPE_DIM), F32)], axis=0)
    return jnp.tile(cos, (1, N_HEADS)), jnp.tile(sin, (1, N_HEADS))


def kernel(x, c, ctx, c_ctx, mod_w, mod_b, norm_mix_w, norm_ffn_w, mla_wq_a, mla_q_norm, mla_wq_b, mla_wkv_a, mla_kv_norm, mla_wkv_b, mla_wo, conv_pw1_w, conv_pw1_b, conv_dw_w, conv_dw_b, conv_norm_w, conv_norm_b, conv_pw2_w, conv_pw2_b, router_w, router_bias, exp_w1, exp_w3, exp_w2, shared_w1, shared_w3, shared_w2, final_norm_w):
    nb, seq, d = x.shape
    n_ctx = ctx.shape[1]
    tm = TOKEN_TILE
    tpb, cpb = seq // tm, n_ctx // tm
    nx, nc = nb * tpb, nb * cpb
    row = lambda a: a.reshape(1, -1)

    pad_rows = (-(nb + 1)) % SUBLANES
    cvec = jnp.concatenate([c, c_ctx[None, :], jnp.zeros((pad_rows, d), F32)], axis=0)
    mod = _modulation(cvec, mod_w, mod_b).reshape(mod_w.shape[0], cvec.shape[0], 6, d)

    lat2d, ctx2d = x.reshape(nb * seq, d), ctx.reshape(nb * n_ctx, d)
    wqb =mla_wq_b[0].reshape(Q_LORA_RANK, N_HEADS, QK_NOPE_DIM + QK_ROPE_DIM)
    wqb = jnp.concatenate([wqb[:, :, :QK_NOPE_DIM].reshape(Q_LORA_RANK, -1),
                           wqb[:, :, QK_NOPE_DIM:].reshape(Q_LORA_RANK, -1)], axis=1)
    wkva = mla_wkv_a[0]
    wkva = jnp.concatenate([wkva[:, :KV_LORA_RANK]]
                           + [wkva[:, KV_LORA_RANK:]] * (LANES // QK_ROPE_DIM), axis=1)
    wkvb = mla_wkv_b[0].reshape(KV_LORA_RANK, N_HEADS, QK_NOPE_DIM + V_HEAD_DIM)
    wkvb = jnp.concatenate([wkvb[:, :, :QK_NOPE_DIM].reshape(KV_LORA_RANK, -1),
                            wkvb[:, :, QK_NOPE_DIM:].reshape(KV_LORA_RANK, -1)], axis=1)
    cos, sin = _rope_tables(seq, tm)
    qn, qp, kcat, v = _pre_mla(
        lat2d, ctx2d, mod[0], row(norm_mix_w[0]), mla_wq_a[0].astype(BF16), row(mla_q_norm[0]),
        wqb.astype(BF16), wkva.astype(BF16), row(mla_kv_norm[0]), wkvb.astype(BF16), cos, sin,
        nb=nb, tpb=tpb, cpb=cpb)
    o = _attention(qn, qp, kcat, v, nb=nb, tpb=tpb, cpb=cpb)

    moe_w = (exp_w1, exp_w3, exp_w2)
    post_w = lambda i: (row(norm_ffn_w[i]), router_w[i].T, router_bias[i].reshape(-1, 1),
                        shared_w1[i].astype(BF16), shared_w3[i].astype(BF16),
                        shared_w2[i].astype(BF16))
    ptile = POST_TILE if seq % POST_TILE == 0 and (nb * n_ctx) % POST_TILE == 0 else tm
    ppb = seq // ptile
    h2, base, idx8, wt8, rank8, cnt = _post(
        o, mla_wo[0].astype(BF16), jnp.zeros((1, d), F32), lat2d, ctx2d, mod[0], *post_w(0),
        tm=ptile, nt=(nb * (seq + n_ctx)) // ptile,
        mrow=lambda i: jnp.where(i < nb * ppb, i // ppb, nb), pair_major=True)
    n_lat = nb * seq
    row_of_lat = lambda t0: t0 // seq
    row_of_any = lambda t0: jnp.where(t0 < n_lat, t0 // seq, nb)
    xs, glu = _moe(h2, base, idx8, wt8, rank8, cnt, mod[0], row(final_norm_w), *moe_w,
                   layer=0, n_out=n_lat + nb * n_ctx, mod_row=row_of_any,
                   tile_quantum=math.gcd(seq, nb * n_ctx), final_norm=False,
                   glu_args=(mod[1], row_of_lat, row(norm_mix_w[1]),
                             conv_pw1_w[0].astype(BF16), row(conv_pw1_b[0]), n_lat))
    taps = jnp.concatenate([conv_dw_w[0], jnp.zeros((1, d), F32)], axis=0)
    act = _conv(glu, taps, row(conv_dw_b[0]), row(conv_norm_w[0]), row(conv_norm_b[0]),
                nt=nx, tpb=tpb)
    h2, base, idx8, wt8, rank8, cnt = _post(
        act, conv_pw2_w[0].astype(BF16), row(conv_pw2_b[0]), xs, xs, mod[1],
        *post_w(1), tm=ptile, nt=nb * ppb, mrow=lambda i: i // ppb, pair_major=False)
    out = _moe(h2, base, idx8, wt8, rank8, cnt, mod[1], row(final_norm_w), *moe_w,
               layer=1, n_out=n_lat, mod_row=row_of_lat, tile_quantum=seq, final_norm=True)
    return out.reshape(nb, seq, d)
```

```python
import functools
import math

import jax
import jax.numpy as jnp
from jax import lax
from jax.experimental import pallas as pl
from jax.experimental.pallas import tpu as pltpu

N_HEADS = 16
QK_NOPE_DIM = 64
QK_ROPE_DIM = 32
V_HEAD_DIM = 64
Q_LORA_RANK = 512
KV_LORA_RANK = 256
ROPE_THETA = 10000.0
GRID_W = 64
CONV_WIDTH = 31
N_EXPERTS = 64
TOP_K = 6
N_GROUPS = 8
TOPK_GROUPS = 4
ROUTED_SCALE = 2.5
NORM_EPS = 1e-6
LOG2_E = 1.4426950408889634

LANES = 128
SUBLANES = 8
TOKEN_TILE = 256
POST_TILE = 512
EXPERT_ROWS = 512
DISPATCH_TILES = (2048, 1024, 512, 256)
COMBINE_TILES = (512, 256)
ISSUE_UNROLL = 4
CONV_HALO = 16
N_PAIRS = N_HEADS // 2
N_ROPE_GROUPS = N_HEADS // 4
VMEM_LIMIT = 56 * 1024 * 1024

F32 = jnp.float32
BF16 = jnp.bfloat16


def _cparams(sem):
    return pltpu.CompilerParams(dimension_semantics=sem, vmem_limit_bytes=VMEM_LIMIT)


def _rms(x, w):
    return x * lax.rsqrt(jnp.mean(x * x, axis=-1, keepdims=True) + NORM_EPS) * w


def _sigmoid(x):
    return 1.0 / (1.0 + jnp.exp(-x))


def _silu(x):
    return x * _sigmoid(x)


def _dot(a, b):
    return jnp.dot(a, b, preferred_element_type=F32)


def _pack_bf16_pairs(x):
    n = x.shape[1] // 2
    hi = lax.bitcast_convert_type(x[:, :n].astype(BF16).astype(F32), jnp.uint32)
    lo = lax.bitcast_convert_type(x[:, n:].astype(BF16).astype(F32), jnp.uint32)
    return hi | (lo >> 16)


def _unpack_bf16_pairs(p):
    hi = lax.bitcast_convert_type(p & jnp.uint32(0xFFFF0000), F32)
    lo = lax.bitcast_convert_type(p << 16, F32)
    return jnp.concatenate([hi, lo], axis=1)


def _store_row_groups(ref, packed):
    m, n = packed.shape
    g = n // LANES
    for c in range(g):
        ref[pl.ds(c, m, stride=g), :] = packed[:, c * LANES:(c + 1) * LANES]


def _load_row_groups(ref, m, g, start=0):
    return jnp.concatenate([ref[pl.ds(start + c, m, stride=g), :] for c in range(g)], axis=1)


def _mod_kernel(c_ref, w_ref, b_ref, o_ref):
    c = c_ref[...]
    o_ref[0] = jnp.dot(_silu(c), w_ref[0], preferred_element_type=F32,
                       precision=lax.Precision.HIGHEST) + b_ref[0]


def _modulation(cvec, mod_w, mod_b):
    depth, d, n = mod_w.shape
    rows = cvec.shape[0]
    tn = 1536
    return pl.pallas_call(
        _mod_kernel,
        grid=(depth, n // tn),
        in_specs=[pl.BlockSpec((rows, d), lambda l, j: (0, 0)),
                  pl.BlockSpec((1, d, tn), lambda l, j: (l, 0, j)),
                  pl.BlockSpec((1, 1, tn), lambda l, j: (l, 0, j))],
        out_specs=pl.BlockSpec((1, rows, tn), lambda l, j: (l, 0, j)),
        out_shape=jax.ShapeDtypeStruct((depth, rows, n), F32),
        compiler_params=_cparams(("arbitrary", "arbitrary")),
        name="modulation",
    )(cvec, mod_w, mod_b.reshape(depth, 1, n))


def _rope(v, cos, sin):
    n = v.shape[1]
    lane = lax.broadcasted_iota(jnp.int32, v.shape, 1)
    up = pltpu.roll(v, 8, axis=1)
    dn = pltpu.roll(v, n - 8, axis=1)
    rot = jnp.where(lane % 16 < 8, -dn, up)
    return v * cos + rot * sin


def _pre_mla_kernel(lat_ref, ctx_ref, mod_ref, nw_ref, wqa_ref, qn_ref, wqb_ref, wkva_ref, kvn_ref,
                    wkvb_ref, cos_ref, sin_ref, qn_out, qp_out, k_out, v_out, *, n_lat_tiles):
    x = jnp.where(pl.program_id(0) < n_lat_tiles, lat_ref[...], ctx_ref[...])
    m = mod_ref[0]
    h = _rms(x, nw_ref[...]) * (1.0 + m[1:2]) + m[0:1]
    hb = h.astype(BF16)
    cos = cos_ref[...]
    sin = sin_ref[...]

    qa = _rms(_dot(hb, wqa_ref[...]), qn_ref[...])
    scale = (QK_NOPE_DIM + QK_ROPE_DIM) ** -0.5 * LOG2_E
    q = _dot(qa.astype(BF16), wqb_ref[...]) * scale
    d_nope = N_HEADS * QK_NOPE_DIM
    q_pe = _rope(q[:, d_nope:], cos, sin)

    kva = _dot(hb, wkva_ref[...])
    kpe4_t = _rope(kva[:, KV_LORA_RANK:], cos[:, :LANES], sin[:, :LANES]).T.astype(BF16)
    ckv = _rms(kva[:, :KV_LORA_RANK], kvn_ref[...])
    kv = _dot(ckv.astype(BF16), wkvb_ref[...])

    for j in range(N_PAIRS):
        sl = slice(j * LANES, (j + 1) * LANES)
        qn_out[j] = q[:, sl].astype(BF16)
        k_out[j, :LANES, :] = kv[:, sl].T.astype(BF16)
        k_out[j, LANES:, :] = kpe4_t
        v_out[j] = kv[:, d_nope + j * LANES:d_nope + (j + 1) * LANES].astype(BF16)
    for g in range(N_ROPE_GROUPS):
        qp_out[g] = q_pe[:, g * LANES:(g + 1) * LANES].astype(BF16)


def _pre_mla(lat, ctx, mod, nw, wqa, qn, wqb, wkva, kvn, wkvb, cos, sin, *, nb, tpb, cpb):
    d = lat.shape[1]
    t = lat.shape[0] + ctx.shape[0]
    tm = TOKEN_TILE
    nx = nb * tpb
    spb = tpb + cpb

    def mrow(i):
        return jnp.where(i < nx, i // tpb, nb)

    def kvblk(i):
        ic = i - nx
        return jnp.where(i < nx, (i // tpb) * spb + cpb + i % tpb, (ic // cpb) * spb + ic % cpb)

    def ropeblk(i):
        return jnp.where(i < nx, i % tpb, tpb)

    full = lambda a: pl.BlockSpec(a.shape, lambda i: (0,) * a.ndim)
    s_rows = nb * spb * tm
    return pl.pallas_call(
        functools.partial(_pre_mla_kernel, n_lat_tiles=nx),
        grid=(t // tm,),
        in_specs=[pl.BlockSpec((tm, d), lambda i: (jnp.minimum(i, nx - 1), 0)),
                  pl.BlockSpec((tm, d), lambda i: (jnp.maximum(i - nx, 0), 0)),
                  pl.BlockSpec((1, 6, d), lambda i: (mrow(i), 0, 0)),
                  full(nw), full(wqa), full(qn), full(wqb), full(wkva), full(kvn), full(wkvb),
                  pl.BlockSpec((tm, cos.shape[1]), lambda i: (ropeblk(i), 0)),
                  pl.BlockSpec((tm, sin.shape[1]), lambda i: (ropeblk(i), 0))],
        out_specs=[pl.BlockSpec((N_PAIRS, tm, LANES), lambda i: (0, i, 0)),
                   pl.BlockSpec((N_ROPE_GROUPS, tm, LANES), lambda i: (0, i, 0)),
                   pl.BlockSpec((N_PAIRS, 2 * LANES, tm), lambda i: (0, 0, kvblk(i))),
                   pl.BlockSpec((N_PAIRS, tm, LANES), lambda i: (0, kvblk(i), 0))],
        out_shape=[jax.ShapeDtypeStruct((N_PAIRS, t, LANES), BF16),
                   jax.ShapeDtypeStruct((N_ROPE_GROUPS, t, LANES), BF16),
                   jax.ShapeDtypeStruct((N_PAIRS, 2 * LANES, s_rows), BF16),
                   jax.ShapeDtypeStruct((N_PAIRS, s_rows, LANES), BF16)],
        compiler_params=_cparams(("arbitrary",)),
        name="pre_mla",
    )(lat, ctx, mod, nw, wqa, qn, wqb, wkva, kvn, wkvb, cos, sin)


def _attn_tile(qn_ref, qp_ref, k_ref, v_ref, o_ref, s_len):
    tq = qn_ref.shape[1]
    lane = lax.broadcasted_iota(jnp.int32, (tq, LANES), 1)

    def pair(j, carry):
        qn = qn_ref[j]
        qp = qp_ref[j // 2]
        k = k_ref[j, :, :s_len]
        v = v_ref[j, :s_len, :]
        outs = []
        for e in range(2):
            qn_m = jnp.where(lane // QK_NOPE_DIM == e, qn, jnp.zeros_like(qn))
            qp_m = jnp.where(lane // QK_ROPE_DIM == 2 * (j % 2) + e, qp, jnp.zeros_like(qp))
            lhs = jnp.concatenate([qn_m, qp_m], axis=1)
            s = _dot(lhs, k)
            p = jnp.exp2(s - jnp.max(s, axis=-1, keepdims=True))
            l = jnp.sum(p, axis=-1, keepdims=True)
            outs.append(_dot(p.astype(BF16), v) * (1.0 / l))
        o_ref[j] = jnp.where(lane < V_HEAD_DIM, outs[0], outs[1]).astype(BF16)
        return carry

    lax.fori_loop(0, N_PAIRS, pair, 0, unroll=8)


def _attn_kernel(qn_ref, qp_ref, k_ref, v_ref, o_ref, *, tpb, n_ctx):
    step = pl.program_id(1)

    @pl.when(step < tpb)
    def _():
        _attn_tile(qn_ref, qp_ref, k_ref, v_ref, o_ref, v_ref.shape[1])

    @pl.when(step >= tpb)
    def _():
        _attn_tile(qn_ref, qp_ref, k_ref, v_ref, o_ref, n_ctx)


def _attention(qn, qp, kcat, v, *, nb, tpb, cpb):
    t = qn.shape[1]
    tq = TOKEN_TILE
    nx = nb * tpb
    s_len = (tpb + cpb) * tq

    def qblk(b, i):
        return jnp.where(i < tpb, b * tpb + i, nx + b * cpb + (i - tpb))

    return pl.pallas_call(
        functools.partial(_attn_kernel, tpb=tpb, n_ctx=cpb * tq),
        grid=(nb, tpb + cpb),
        in_specs=[pl.BlockSpec((N_PAIRS, tq, LANES), lambda b, i: (0, qblk(b, i), 0)),
                  pl.BlockSpec((N_ROPE_GROUPS, tq, LANES), lambda b, i: (0, qblk(b, i), 0)),
                  pl.BlockSpec((N_PAIRS, 2 * LANES, s_len), lambda b, i: (0, 0, b)),
                  pl.BlockSpec((N_PAIRS, s_len, LANES), lambda b, i: (0, b, 0))],
        out_specs=pl.BlockSpec((N_PAIRS, tq, LANES), lambda b, i: (0, qblk(b, i), 0)),
        out_shape=jax.ShapeDtypeStruct((N_PAIRS, t, LANES), BF16),
        compiler_params=_cparams(("arbitrary", "arbitrary")),
        name="attention",
    )(qn, qp, kcat, v)


def _post_kernel(act_ref, wmix_ref, bmix_ref, xs_ref, xs2_ref, mod_ref, nfw_ref, rwt_ref, rb_ref,
                 sw1_ref, sw3_ref, sw2_ref,
                 h2_out, base_out, idx_out, wt_out, rank_out, cnt_out, cnt_scr,
                 *, pair_major, n_first, n_tiles):
    i = pl.program_id(0)
    tm = xs_ref.shape[0]
    if n_first >= n_tiles:
        xs_in = xs_ref[...]
    else:
        xs_in = jnp.where(i < n_first, xs_ref[...], xs2_ref[...])

    @pl.when(i == 0)
    def _():
        cnt_scr[...] = jnp.zeros_like(cnt_scr)

    if pair_major:
        act = jnp.concatenate([act_ref[j] for j in range(N_PAIRS)], axis=1)
    else:
        act = act_ref[...]
    m = mod_ref[0]
    mix = _dot(act, wmix_ref[...]) + bmix_ref[...]
    xs = xs_in + m[2:3] * mix
    h2 = _rms(xs, nfw_ref[...]) * (1.0 + m[4:5]) + m[3:4]
    _store_row_groups(h2_out, _pack_bf16_pairs(h2))

    h2b = h2.astype(BF16)
    hid = _silu(_dot(h2b, sw1_ref[...])) * _dot(h2b, sw3_ref[...])
    base_out[...] = xs + m[5:6] * _dot(hid.astype(BF16), sw2_ref[...])

    logits = lax.dot_general(rwt_ref[...], h2, (((1,), (1,)), ((), ())),
                             preferred_element_type=F32, precision=lax.Precision.HIGHEST)
    scores = _sigmoid(logits)
    sel = scores + rb_ref[...]
    per_group = N_EXPERTS // N_GROUPS
    neg = jnp.float32(-jnp.inf)
    sub_iota = lax.broadcasted_iota(jnp.int32, (per_group, tm), 0)
    gs_rows = []
    for g in range(N_GROUPS):
        sg = sel[g * per_group:(g + 1) * per_group, :]
        m1 = jnp.max(sg, axis=0, keepdims=True)
        first = jnp.min(jnp.where(sg == m1, sub_iota, per_group), axis=0, keepdims=True)
        m2 = jnp.max(jnp.where(sub_iota == first, neg, sg), axis=0, keepdims=True)
        gs_rows.append(m1 + m2)
    gs = jnp.concatenate(gs_rows, axis=0)
    g_iota = lax.broadcasted_iota(jnp.int32, (N_GROUPS, tm), 0)
    gmask = jnp.zeros((N_GROUPS, tm), F32)
    work = gs
    for _ in range(TOPK_GROUPS):
        mx = jnp.max(work, axis=0, keepdims=True)
        gi = jnp.min(jnp.where(work == mx, g_iota, N_GROUPS), axis=0, keepdims=True)
        pick = g_iota == gi
        gmask = jnp.where(pick, 1.0, gmask)
        work = jnp.where(pick, neg, work)
    emask = jnp.concatenate(
        [jnp.broadcast_to(gmask[g:g + 1, :], (per_group, tm)) for g in range(N_GROUPS)], axis=0)
    masked = jnp.where(emask > 0.5, sel, neg)
    e_iota = lax.broadcasted_iota(jnp.int32, (N_EXPERTS, tm), 0)
    picks, idx_rows, w_rows = [], [], []
    for _ in range(TOP_K):
        mx = jnp.max(masked, axis=0, keepdims=True)
        ei = jnp.min(jnp.where(masked == mx, e_iota, N_EXPERTS), axis=0, keepdims=True)
        pick = e_iota == ei
        picks.append(pick)
        idx_rows.append(ei)
        w_rows.append(jnp.sum(jnp.where(pick, scores, 0.0), axis=0, keepdims=True))
        masked = jnp.where(pick, neg, masked)
    wsum = w_rows[0]
    for r in w_rows[1:]:
        wsum = wsum + r
    wnorm = ROUTED_SCALE / wsum

    member = jnp.zeros((N_EXPERTS, tm), F32)
    for pick in picks:
        member = jnp.where(pick, 1.0, member)
    before = (lax.broadcasted_iota(jnp.int32, (tm, tm), 0)
              < lax.broadcasted_iota(jnp.int32, (tm, tm), 1))
    prefix = _dot(member.astype(BF16), jnp.where(before, 1.0, 0.0).astype(BF16))
    cnt = cnt_scr[...]
    rank_full = cnt[:, 0:1] + prefix
    rank_rows = [jnp.sum(jnp.where(pick, rank_full, 0.0), axis=0, keepdims=True)
                 for pick in picks]
    cnt = cnt + jnp.sum(member, axis=1, keepdims=True)
    cnt_scr[...] = cnt
    cnt_out[...] = cnt

    pad = SUBLANES - TOP_K
    zi = jnp.zeros((pad, tm), jnp.int32)
    zf = jnp.zeros((pad, tm), F32)
    idx_out[...] = jnp.concatenate(idx_rows + [zi], axis=0)
    wt_out[...] = jnp.concatenate([r * wnorm for r in w_rows] + [zf], axis=0)
    rank_out[...] = jnp.concatenate([r.astype(jnp.int32) for r in rank_rows] + [zi], axis=0)


def _post(act, wmix, bmix, xs, xs2, mod, nfw, rwt, rb, sw1, sw3, sw2, *, tm, nt, mrow,
          pair_major):
    d = xs.shape[1]
    t = nt * tm
    g = d // 2 // LANES
    n_first = min(nt, xs.shape[0] // tm)
    full = lambda a: pl.BlockSpec(a.shape, lambda i: (0,) * a.ndim)
    if pair_major:
        act_spec = pl.BlockSpec((N_PAIRS, tm, LANES), lambda i: (0, i, 0))
    else:
        act_spec = pl.BlockSpec((tm, act.shape[1]), lambda i: (i, 0))
    row8 = pl.BlockSpec((SUBLANES, tm), lambda i: (0, i))
    return pl.pallas_call(
        functools.partial(_post_kernel, pair_major=pair_major, n_first=n_first, n_tiles=nt),
        grid=(nt,),
        in_specs=[act_spec, full(wmix), full(bmix),
                  pl.BlockSpec((tm, d), lambda i: (jnp.minimum(i, n_first - 1), 0)),
                  pl.BlockSpec((tm, d), lambda i: (jnp.maximum(i - n_first, 0), 0)),
                  pl.BlockSpec((1, 6, d), lambda i: (mrow(i), 0, 0)),
                  full(nfw), full(rwt), full(rb), full(sw1), full(sw3), full(sw2)],
        out_specs=[pl.BlockSpec((tm * g, LANES), lambda i: (i, 0)),
                   pl.BlockSpec((tm, d), lambda i: (i, 0)),
                   row8, row8, row8,
                   pl.BlockSpec((N_EXPERTS, LANES), lambda i: (0, 0))],
        out_shape=[jax.ShapeDtypeStruct((t * g, LANES), jnp.uint32),
                   jax.ShapeDtypeStruct((t, d), F32),
                   jax.ShapeDtypeStruct((SUBLANES, t), jnp.int32),
                   jax.ShapeDtypeStruct((SUBLANES, t), F32),
                   jax.ShapeDtypeStruct((SUBLANES, t), jnp.int32),
                   jax.ShapeDtypeStruct((N_EXPERTS, LANES), F32)],
        scratch_shapes=[pltpu.VMEM((N_EXPERTS, LANES), F32)],
        compiler_params=_cparams(("arbitrary",)),
        name="post_mixer",
    )(act, wmix, bmix, xs, xs2, mod, nfw, rwt, rb, sw1, sw3, sw2)


def _dispatch_kernel(dest_ref, fill_ref, h2_ref, xg_out, zbuf, sem, zsem, *, t_total, tile, g):
    step = pl.program_id(0)
    base = step * tile
    rows = zbuf.shape[0]
    n_blocks = xg_out.shape[0] // rows

    @pl.when(step == 0)
    def _():
        zbuf[...] = jnp.zeros_like(zbuf)

        def fill_copy(n):
            return pltpu.make_async_copy(
                zbuf, xg_out.at[pl.ds(pl.multiple_of(n * rows, rows), rows)], zsem)

        def start(n, carry):
            @pl.when(fill_ref[n] > 0)
            def _():
                fill_copy(n).start()
            return carry

        def wait(n, carry):
            @pl.when(fill_ref[n] > 0)
            def _():
                fill_copy(n).wait()
            return carry

        lax.fori_loop(0, n_blocks, start, 0)
        lax.fori_loop(0, n_blocks, wait, 0)

    def row_copy(r, slot):
        return pltpu.make_async_copy(h2_ref.at[pl.ds(pl.multiple_of(r * g, g), g)],
                                     xg_out.at[pl.ds(pl.multiple_of(slot * g, g), g)], sem)

    def issue(r, carry):
        for k in range(TOP_K):
            row_copy(r, dest_ref[k * t_total + base + r]).start(priority=k % 2)
        return carry

    lax.fori_loop(0, tile, issue, 0, unroll=ISSUE_UNROLL)

    def drain(r, carry):
        for k in range(TOP_K):
            row_copy(0, 0).wait()
        return carry

    lax.fori_loop(0, tile, drain, 0, unroll=ISSUE_UNROLL)


def _dispatch(dest, fill, h2, n_slots, g):
    t = h2.shape[0] // g
    tile = next(c for c in DISPATCH_TILES if t % c == 0)
    return pl.pallas_call(
        functools.partial(_dispatch_kernel, t_total=t, tile=tile, g=g),
        grid_spec=pltpu.PrefetchScalarGridSpec(
            num_scalar_prefetch=2,
            grid=(t // tile,),
            in_specs=[pl.BlockSpec((tile * g, LANES), lambda i, dr, fr: (i, 0))],
            out_specs=pl.BlockSpec(memory_space=pl.ANY),
            scratch_shapes=[pltpu.VMEM((EXPERT_ROWS * g, LANES), h2.dtype),
                            pltpu.SemaphoreType.DMA(()), pltpu.SemaphoreType.DMA(())]),
        out_shape=jax.ShapeDtypeStruct((n_slots * g, LANES), h2.dtype),
        compiler_params=_cparams(("arbitrary",)),
        name="moe_dispatch",
    )(dest, fill, h2)


def _expert_kernel(be_ref, nu_ref, first_ref, slot_ref, nxt_ref, x_ref, w1_hbm, w3_hbm, w2_hbm,
                   y_ref, wf1, wf3, wf2, w1b, w3b, w2b, sems, *, g, layer):
    n = pl.program_id(0)
    used = n < nu_ref[0]
    rows = x_ref.shape[0] // g

    def weight_copies(e, s):
        return (pltpu.make_async_copy(w1_hbm.at[layer, e], wf1.at[s], sems.at[s]),
                pltpu.make_async_copy(w3_hbm.at[layer, e], wf3.at[s], sems.at[s]),
                pltpu.make_async_copy(w2_hbm.at[layer, e], wf2.at[s], sems.at[s]))

    @pl.when(n == 0)
    def _():
        for c in weight_copies(be_ref[0], 0):
            c.start()

    @pl.when(first_ref[n] > 0)
    def _():
        s = slot_ref[n]
        for c in weight_copies(be_ref[n], s):
            c.wait()
        w1b[...] = wf1[s].astype(BF16)
        w3b[...] = wf3[s].astype(BF16)
        w2b[...] = wf2[s].astype(BF16)

        @pl.when(nxt_ref[n] >= 0)
        def _():
            for c in weight_copies(nxt_ref[n], 1 - s):
                c.start()

    @pl.when(used)
    def _():
        x = _unpack_bf16_pairs(_load_row_groups(x_ref, rows, g)).astype(BF16)
        hid = _silu(_dot(x, w1b[...])) * _dot(x, w3b[...])
        _store_row_groups(y_ref, _pack_bf16_pairs(_dot(hid.astype(BF16), w2b[...])))

    @pl.when(jnp.logical_not(used))
    def _():
        y_ref[...] = jnp.zeros_like(y_ref)


def _experts(blk_e, n_used, xg, w1, w3, w2, g, layer):
    rows = EXPERT_ROWS * g
    nblk = xg.shape[0] // rows
    _, _, d, e_dim = w1.shape

    pos = jnp.arange(nblk, dtype=jnp.int32)
    first = jnp.logical_or(pos == 0, blk_e != jnp.roll(blk_e, 1))
    slot = (jnp.cumsum(first.astype(jnp.int32)) - 1) % 2
    next_first = lax.cummin(jnp.where(first, pos, nblk), reverse=True)
    next_first = jnp.concatenate([next_first[1:], jnp.full((1,), nblk, jnp.int32)])
    nxt = jnp.where(next_first < nblk, jnp.take(blk_e, jnp.minimum(next_first, nblk - 1)), -1)

    def xmap(n, be, nu, fr, sr, nr):
        return (jnp.minimum(n, nu[0] - 1), 0)

    return pl.pallas_call(
        functools.partial(_expert_kernel, g=g, layer=layer),
        grid_spec=pltpu.PrefetchScalarGridSpec(
            num_scalar_prefetch=5,
            grid=(nblk,),
            in_specs=[pl.BlockSpec((rows, LANES), xmap),
                      pl.BlockSpec(memory_space=pl.ANY),
                      pl.BlockSpec(memory_space=pl.ANY),
                      pl.BlockSpec(memory_space=pl.ANY)],
            out_specs=pl.BlockSpec((rows, LANES), lambda n, be, nu, fr, sr, nr: (n, 0)),
            scratch_shapes=[pltpu.VMEM((2, d, e_dim), F32), pltpu.VMEM((2, d, e_dim), F32),
                            pltpu.VMEM((2, e_dim, d), F32),
                            pltpu.VMEM((d, e_dim), BF16), pltpu.VMEM((d, e_dim), BF16),
                            pltpu.VMEM((e_dim, d), BF16),
                            pltpu.SemaphoreType.DMA((2,))]),
        out_shape=jax.ShapeDtypeStruct(xg.shape, jnp.uint32),
        compiler_params=_cparams(("arbitrary",)),
        name="moe_experts",
    )(blk_e, n_used, first.astype(jnp.int32), slot.astype(jnp.int32), nxt.astype(jnp.int32),
      xg, w1, w3, w2)


def _combine_kernel(dest_ref, y_hbm, base_ref, wt_ref, mod_ref, fnw_ref, *rest,
                    t_total, n_tiles, final_norm, g, n_glu_tiles):
    if n_glu_tiles:
        mod1_ref, nw1_ref, pw_ref, pb_ref, o_ref, glu_ref, buf, sems = rest
    else:
        o_ref, buf, sems = rest
    tm = base_ref.shape[0]
    step = pl.program_id(0)
    cur = step % 2
    half = TOP_K * tm * g
    chunk = SUBLANES
    gate = mod_ref[0][5:6]
    fnw = fnw_ref[...]

    def row_copy(slot, which, k, r):
        dst = pl.multiple_of(which * half + (k * tm + r) * g, g)
        return pltpu.make_async_copy(y_hbm.at[pl.ds(pl.multiple_of(slot * g, g), g)],
                                     buf.at[pl.ds(dst, g)], sems.at[which])

    def issue_chunk(tile, which, r0):
        for dr in range(chunk):
            for k in range(TOP_K):
                slot = dest_ref[k * t_total + tile * tm + r0 + dr]
                row_copy(slot, which, k, r0 + dr).start(priority=k % 2)

    def sum_chunk(r0):
        wt = wt_ref[pl.ds(r0, chunk), :]
        acc = None
        for k in range(TOP_K):
            start = cur * half + (k * tm + r0) * g
            packed = _load_row_groups(buf, chunk, g, start=start)
            term = wt[:, k:k + 1] * _unpack_bf16_pairs(packed)
            acc = term if acc is None else acc + term
        out = base_ref[pl.ds(r0, chunk), :] + gate * acc
        if final_norm:
            out = _rms(out, fnw)
        o_ref[pl.ds(r0, chunk), :] = out

    @pl.when(step == 0)
    def _():
        def first(c, carry):
            issue_chunk(0, 0, pl.multiple_of(c * chunk, chunk))
            return carry
        lax.fori_loop(0, tm // chunk, first, 0)

    def drain(r, carry):
        for k in range(TOP_K):
            row_copy(0, cur, k, 0).wait()
        return carry

    lax.fori_loop(0, tm, drain, 0, unroll=ISSUE_UNROLL)

    @pl.when(step + 1 < n_tiles)
    def _():
        def body(c, carry):
            r0 = pl.multiple_of(c * chunk, chunk)
            sum_chunk(r0)
            issue_chunk(step + 1, 1 - cur, r0)
            return carry
        lax.fori_loop(0, tm // chunk, body, 0, unroll=2)

    @pl.when(step + 1 == n_tiles)
    def _():
        def body(c, carry):
            sum_chunk(pl.multiple_of(c * chunk, chunk))
            return carry
        lax.fori_loop(0, tm // chunk, body, 0)

    if n_glu_tiles:
        @pl.when(step < n_glu_tiles)
        def _():
            _pre_conv_kernel(o_ref, mod1_ref, nw1_ref, pw_ref, pb_ref, glu_ref)


def _combine(dest, yg, base, wt_t, mod, fnw, *, tm, n_out, mod_row, final_norm, g,
             glu_args=None):
    d = base.shape[1]
    nt = n_out // tm
    t = nt * tm
    const = lambda a: pl.BlockSpec(a.shape, lambda i, dr: (0,) * a.ndim)
    in_specs = [pl.BlockSpec(memory_space=pl.ANY),
                pl.BlockSpec((tm, d), lambda i, dr: (i, 0)),
                pl.BlockSpec((tm, SUBLANES), lambda i, dr: (i, 0)),
                pl.BlockSpec((1, 6, d), lambda i, dr: (mod_row(i * tm), 0, 0)),
                pl.BlockSpec((1, d), lambda i, dr: (0, 0))]
    out_specs = [pl.BlockSpec((tm, d), lambda i, dr: (i, 0))]
    out_shape = [jax.ShapeDtypeStruct((t, d), F32)]
    args = [dest, yg, base, wt_t, mod, fnw]
    n_glu = 0
    if glu_args is not None:
        mod1, mod_row1, nw1, pw, pb, n_glu_tokens = glu_args
        n_glu = n_glu_tokens // tm
        last = n_glu - 1
        in_specs += [pl.BlockSpec((1, 6, d),
                                  lambda i, dr: (mod_row1(jnp.minimum(i, last) * tm), 0, 0)),
                     const(nw1), const(pw), const(pb)]
        out_specs.append(pl.BlockSpec((tm, d), lambda i, dr: (jnp.minimum(i, last), 0)))
        out_shape.append(jax.ShapeDtypeStruct((n_glu * tm, d), F32))
        args += [mod1, nw1, pw, pb]
    res = pl.pallas_call(
        functools.partial(_combine_kernel, t_total=base.shape[0], n_tiles=nt,
                          final_norm=final_norm, g=g, n_glu_tiles=n_glu),
        grid_spec=pltpu.PrefetchScalarGridSpec(
            num_scalar_prefetch=1,
            grid=(nt,),
            in_specs=in_specs,
            out_specs=out_specs,
            scratch_shapes=[pltpu.VMEM((2 * TOP_K * tm * g, LANES), jnp.uint32),
                            pltpu.SemaphoreType.DMA((2,))]),
        out_shape=out_shape,
        compiler_params=_cparams(("arbitrary",)),
        name="moe_combine",
    )(*args)
    return res if glu_args is not None else res[0]


def _moe(h2, base, idx8, wt8, rank8, cnt, mod, fnw, w1, w3, w2, *, layer, n_out, mod_row,
         tile_quantum, final_norm, glu_args=None):
    g = base.shape[1] // 2 // LANES
    t = h2.shape[0] // g
    rows = EXPERT_ROWS
    counts = cnt[:, 0].astype(jnp.int32)
    padded = (counts + rows - 1) // rows * rows
    pad_end = jnp.cumsum(padded)
    pad_start = pad_end - padded
    nblk = (t * TOP_K + rows - 1) // rows + N_EXPERTS
    e_ids = jnp.arange(N_EXPERTS, dtype=jnp.int32)
    idx = idx8[:TOP_K]
    start_of = jnp.sum(jnp.where(idx[:, :, None] == e_ids, pad_start, 0), axis=-1)
    dest = (start_of + rank8[:TOP_K]).reshape(-1)
    n_used = (pad_end[-1] // rows).astype(jnp.int32)
    blk_start = jnp.arange(nblk, dtype=jnp.int32) * rows
    e_of_blk = jnp.sum((pad_end[None, :] <= blk_start[:, None]).astype(jnp.int32), axis=1)
    valid_end = jnp.sum(jnp.where(e_of_blk[:, None] == e_ids, pad_start + counts, 0), axis=1)
    fill = (blk_start + rows > valid_end).astype(jnp.int32)
    e_last = jnp.max(jnp.where(counts > 0, e_ids, 0))
    blk_e = jnp.minimum(e_of_blk, e_last).astype(jnp.int32)
    xg = _dispatch(dest, fill, h2, nblk * rows, g)
    yg = _experts(blk_e, n_used.reshape(1), xg, w1, w3, w2, g, layer)
    tm = next(c for c in COMBINE_TILES if tile_quantum % c == 0 and n_out % c == 0)
    return _combine(dest, yg, base, wt8.T, mod, fnw, tm=tm, n_out=n_out, mod_row=mod_row,
                    final_norm=final_norm, g=g, glu_args=glu_args)


def _pre_conv_kernel(xs_ref, mod_ref, nw_ref, w_ref, b_ref, o_ref):
    m = mod_ref[0]
    h = _rms(xs_ref[...], nw_ref[...]) * (1.0 + m[1:2]) + m[0:1]
    u = _dot(h.astype(BF16), w_ref[...]) + b_ref[...]
    d = o_ref.shape[1]
    o_ref[...] = u[:, :d] * _sigmoid(u[:, d:])


def _conv_kernel(prev_ref, cur_ref, next_ref, dw_ref, db_ref, lw_ref, lb_ref, o_ref, win, conv,
                 *, tpb):
    tm, d = cur_ref.shape
    r = pl.program_id(0) % tpb
    halo = CONV_HALO
    pad = CONV_WIDTH // 2
    win[0:halo, :] = jnp.where(r > 0, prev_ref[...], 0.0)
    win[halo:halo + tm, :] = cur_ref[...]
    win[halo + tm:halo + tm + halo, :] = jnp.where(r < tpb - 1, next_ref[...], 0.0)

    rows = 64
    for c in range(d // LANES):
        cs = slice(c * LANES, (c + 1) * LANES)
        taps = dw_ref[:, cs]
        bias = db_ref[:, cs]

        def chunk(q, carry):
            r0 = pl.multiple_of(q * rows, rows)
            slab = win[pl.ds(r0, rows + 2 * halo), cs]
            acc = jnp.zeros((rows, LANES), F32) + bias
            for b in range(SUBLANES):
                part = None
                for j in range(CONV_WIDTH):
                    off = halo - pad + j
                    if off % SUBLANES == b:
                        a = off - b
                        term = taps[j:j + 1, :] * slab[a:a + rows + SUBLANES, :]
                        part = term if part is None else part + term
                acc = acc + part[b:b + rows, :]
            conv[pl.ds(r0, rows), cs] = acc
            return carry

        lax.fori_loop(0, tm // rows, chunk, 0)

    u = conv[...]
    mu = jnp.mean(u, axis=-1, keepdims=True)
    var = jnp.mean(jnp.square(u - mu), axis=-1, keepdims=True)
    y = (u - mu) * lax.rsqrt(var + NORM_EPS) * lw_ref[...] + lb_ref[...]
    o_ref[...] = _silu(y).astype(BF16)


def _conv(glu, dw, db, lw, lb, *, nt, tpb):
    d = glu.shape[1]
    tm = TOKEN_TILE
    hb = tm // CONV_HALO
    n_halo_blocks = glu.shape[0] // CONV_HALO
    full = lambda a: pl.BlockSpec(a.shape, lambda i: (0,) * a.ndim)
    return pl.pallas_call(
        functools.partial(_conv_kernel, tpb=tpb),
        grid=(nt,),
        in_specs=[pl.BlockSpec((CONV_HALO, d), lambda i: (jnp.maximum(i * hb - 1, 0), 0)),
                  pl.BlockSpec((tm, d), lambda i: (i, 0)),
                  pl.BlockSpec((CONV_HALO, d),
                               lambda i: (jnp.minimum((i + 1) * hb, n_halo_blocks - 1), 0)),
                  full(dw), full(db), full(lw), full(lb)],
        out_specs=pl.BlockSpec((tm, d), lambda i: (i, 0)),
        out_shape=jax.ShapeDtypeStruct((nt * tm, d), BF16),
        scratch_shapes=[pltpu.VMEM((tm + 2 * CONV_HALO, d), F32), pltpu.VMEM((tm, d), F32)],
        compiler_params=_cparams(("arbitrary",)),
        name="dw_conv",
    )(glu, glu, glu, dw, db, lw, lb)


def _rope_tables(n_tokens, extra_rows):
    rows = n_tokens // GRID_W
    row = jnp.repeat(jnp.arange(rows, dtype=F32), GRID_W)
    col = jnp.tile(jnp.arange(GRID_W, dtype=F32), rows)
    axis_dim = QK_ROPE_DIM // 2
    inv_freq = ROPE_THETA ** (-jnp.arange(0, axis_dim, 2, dtype=F32) / axis_dim)
    ang_r = row[:, None] * inv_freq
    ang_c = col[:, None] * inv_freq
    ang = jnp.concatenate([ang_r, ang_r, ang_c, ang_c], axis=-1)
    cos = jnp.concatenate([jnp.cos(ang), jnp.ones((extra_rows, QK_ROPE_DIM), F32)], axis=0)
    sin = jnp.concatenate([jnp.sin(ang), jnp.zeros((extra_rows, QK_ROPE_DIM), F32)], axis=0)
    return jnp.tile(cos, (1, N_HEADS)), jnp.tile(sin, (1, N_HEADS))


def kernel(x, c, ctx, c_ctx, mod_w, mod_b, norm_mix_w, norm_ffn_w, mla_wq_a, mla_q_norm, mla_wq_b, mla_wkv_a, mla_kv_norm, mla_wkv_b, mla_wo, conv_pw1_w, conv_pw1_b, conv_dw_w, conv_dw_b, conv_norm_w, conv_norm_b, conv_pw2_w, conv_pw2_b, router_w, router_bias, exp_w1, exp_w3, exp_w2, shared_w1, shared_w3, shared_w2, final_norm_w):
    nb, seq, d = x.shape
    n_ctx = ctx.shape[1]
    tm = TOKEN_TILE
    tpb, cpb = seq // tm, n_ctx // tm
    nx, nc = nb * tpb, nb * cpb
    row = lambda a: a.reshape(1, -1)

    pad_rows = (-(nb + 1)) % SUBLANES
    cvec = jnp.concatenate([c, c_ctx[None, :], jnp.zeros((pad_rows, d), F32)], axis=0)
    mod = _modulation(cvec, mod_w, mod_b).reshape(mod_w.shape[0], cvec.shape[0], 6, d)

    lat2d, ctx2d = x.reshape(nb * seq, d), ctx.reshape(nb * n_ctx, d)
    wqb =mla_wq_b[0].reshape(Q_LORA_RANK, N_HEADS, QK_NOPE_DIM + QK_ROPE_DIM)
    wqb = jnp.concatenate([wqb[:, :, :QK_NOPE_DIM].reshape(Q_LORA_RANK, -1),
                           wqb[:, :, QK_NOPE_DIM:].reshape(Q_LORA_RANK, -1)], axis=1)
    wkva = mla_wkv_a[0]
    wkva = jnp.concatenate([wkva[:, :KV_LORA_RANK]]
                           + [wkva[:, KV_LORA_RANK:]] * (LANES // QK_ROPE_DIM), axis=1)
    wkvb = mla_wkv_b[0].reshape(KV_LORA_RANK, N_HEADS, QK_NOPE_DIM + V_HEAD_DIM)
    wkvb = jnp.concatenate([wkvb[:, :, :QK_NOPE_DIM].reshape(KV_LORA_RANK, -1),
                            wkvb[:, :, QK_NOPE_DIM:].reshape(KV_LORA_RANK, -1)], axis=1)
    cos, sin = _rope_tables(seq, tm)
    qn, qp, kcat, v = _pre_mla(
        lat2d, ctx2d, mod[0], row(norm_mix_w[0]), mla_wq_a[0].astype(BF16), row(mla_q_norm[0]),
        wqb.astype(BF16), wkva.astype(BF16), row(mla_kv_norm[0]), wkvb.astype(BF16), cos, sin,
        nb=nb, tpb=tpb, cpb=cpb)
    o = _attention(qn, qp, kcat, v, nb=nb, tpb=tpb, cpb=cpb)

    moe_w = (exp_w1, exp_w3, exp_w2)
    post_w = lambda i: (row(norm_ffn_w[i]), router_w[i].T, router_bias[i].reshape(-1, 1),
                        shared_w1[i].astype(BF16), shared_w3[i].astype(BF16),
                        shared_w2[i].astype(BF16))
    ptile = POST_TILE if seq % POST_TILE == 0 and (nb * n_ctx) % POST_TILE == 0 else tm
    ppb = seq // ptile
    h2, base, idx8, wt8, rank8, cnt = _post(
        o, mla_wo[0].astype(BF16), jnp.zeros((1, d), F32), lat2d, ctx2d, mod[0], *post_w(0),
        tm=ptile, nt=(nb * (seq + n_ctx)) // ptile,
        mrow=lambda i: jnp.where(i < nb * ppb, i // ppb, nb), pair_major=True)
    n_lat = nb * seq
    row_of_lat = lambda t0: t0 // seq
    row_of_any = lambda t0: jnp.where(t0 < n_lat, t0 // seq, nb)
    xs, glu = _moe(h2, base, idx8, wt8, rank8, cnt, mod[0], row(final_norm_w), *moe_w,
                   layer=0, n_out=n_lat + nb * n_ctx, mod_row=row_of_any,
                   tile_quantum=math.gcd(seq, nb * n_ctx), final_norm=False,
                   glu_args=(mod[1], row_of_lat, row(norm_mix_w[1]),
                             conv_pw1_w[0].astype(BF16), row(conv_pw1_b[0]), n_lat))
    taps = jnp.concatenate([conv_dw_w[0], jnp.zeros((1, d), F32)], axis=0)
    act = _conv(glu, taps, row(conv_dw_b[0]), row(conv_norm_w[0]), row(conv_norm_b[0]),
                nt=nx, tpb=tpb)
    h2, base, idx8, wt8, rank8, cnt = _post(
        act, conv_pw2_w[0].astype(BF16), row(conv_pw2_b[0]), xs, xs, mod[1],
        *post_w(1), tm=ptile, nt=nb * ppb, mrow=lambda i: i // ppb, pair_major=False)
    out = _moe(h2, base, idx8, wt8, rank8, cnt, mod[1], row(final_norm_w), *moe_w,
               layer=1, n_out=n_lat, mod_row=row_of_lat, tile_quantum=seq, final_norm=True)
    return out.reshape(nb, seq, d)
```
